```python
import math
import jax, jax.numpy as jnp
from jax import lax
import numpy as np

D_MODEL = 4096
BATCH = 2
SEQ = 8192
DEPTH = 2

CHUNK = 64
D_MIX = D_MODEL
D_A = D_MIX // 2
D_B = D_MIX - D_A
SGU_BLOCK = 128
SGU_GROUPS = 8
SGU_GW = D_A // SGU_GROUPS
DA_HEADS = 8
DA_HD = D_B // DA_HEADS // 2
ROT_DIM = DA_HD // 4
ROPE_THETA = 500000.0
Q_BLOCK = 128
EPS = 1e-5
SPLITS = (D_A, D_A, D_A, D_B, D_B, D_B, D_B)
D_IN = sum(SPLITS)

kernel_name = "hybrid_gmlp_diffattn_chunk_causal"


def rmsnorm(x, g):
    xf = x.astype(jnp.float32)
    y = xf * lax.rsqrt(jnp.mean(xf * xf, axis=-1, keepdims=True) + EPS)
    return (y * g.astype(jnp.float32)).astype(x.dtype)


def layernorm(x, g, b):
    xf = x.astype(jnp.float32)
    mu = jnp.mean(xf, axis=-1, keepdims=True)
    xc = xf - mu
    var = jnp.mean(xc * xc, axis=-1, keepdims=True)
    y = xc * lax.rsqrt(var + EPS) * g.astype(jnp.float32) + b.astype(jnp.float32)
    return y.astype(x.dtype)


def lambda_init_fn(layer_idx):
    return 0.8 - 0.6 * math.exp(-0.3 * layer_idx)


def rope_tables(positions):
    inv_freq = ROPE_THETA ** (-jnp.arange(0, ROT_DIM, 2, dtype=jnp.float32) / ROT_DIM)
    ang = positions.astype(jnp.float32)[..., None] * inv_freq
    return jnp.cos(ang)[:, :, None, None, :], jnp.sin(ang)[:, :, None, None, :]


def partial_rope(t, cos, sin):
    tf = t.astype(jnp.float32)
    half = ROT_DIM // 2
    x1 = tf[..., :half]
    x2 = tf[..., half:ROT_DIM]
    rot = jnp.concatenate([x1 * cos - x2 * sin, x2 * cos + x1 * sin], axis=-1)
    return jnp.concatenate([rot, tf[..., ROT_DIM:]], axis=-1).astype(t.dtype)


def sgu_mix(v, w_s, b_s):
    B, S, _ = v.shape
    nc = S // SGU_BLOCK
    vb = v.reshape(B, nc, SGU_BLOCK, SGU_GROUPS, SGU_GW)
    idx = jnp.arange(SGU_BLOCK)
    mask = (idx[None, :] // CHUNK) <= (idx[:, None] // CHUNK)
    w = jnp.where(mask[None], w_s, jnp.zeros_like(w_s))
    out = jnp.einsum('gij,bcjge->bcige', w, vb) + b_s.T[None, None, :, :, None]
    return out.reshape(B, S, D_A)


def diff_attention(q, k, v, lam):
    B, S, H, _, d = q.shape
    nb = S // Q_BLOCK
    q = q * jnp.asarray(d ** -0.5, q.dtype)
    qb = q.reshape(B, nb, Q_BLOCK, H, 2, d).transpose(1, 0, 3, 4, 2, 5)
    kt = k.transpose(0, 2, 3, 1, 4)
    vt = v.transpose(0, 2, 1, 3).astype(jnp.float32)
    k_chunk = jnp.arange(S) // CHUNK
    neg = jnp.finfo(jnp.float32).min

    def block(args):
        qi, bi = args
        s = jnp.einsum('bhcqd,bhckd->bhcqk', qi, kt).astype(jnp.float32)
        q_chunk = (bi * Q_BLOCK + jnp.arange(Q_BLOCK)) // CHUNK
        mask = k_chunk[None, :] <= q_chunk[:, None]
        s = jnp.where(mask, s, neg)
        p = jax.nn.softmax(s, axis=-1)
        a = p[:, :, 0] - lam * p[:, :, 1]
        o = jnp.einsum('bhqk,bhke->bhqe', a, vt)
        return o.astype(v.dtype)

    out = lax.map(block, (qb, jnp.arange(nb)))
    return out.transpose(1, 0, 3, 2, 4).reshape(B, S, H, 2 * d)


def setup_inputs(seed: int = 0) -> dict:
    key = jax.random.key(seed)
    ks = jax.random.split(key, 16)
    f32 = jnp.float32
    x = jax.random.normal(ks[0], (BATCH, SEQ, D_MODEL), f32)
    offset = jax.random.randint(ks[1], (BATCH, 1), 0, 100000, dtype=jnp.int32)
    positions = offset + jnp.arange(SEQ, dtype=jnp.int32)[None, :]
    norm_g = 1.0 + 0.02 * jax.random.normal(ks[2], (DEPTH, D_MODEL), f32)
    w_in = jax.random.normal(ks[3], (DEPTH, D_MODEL, D_IN), f32) * (D_MODEL ** -0.5)
    ln_g = 1.0 + 0.02 * jax.random.normal(ks[4], (DEPTH, D_A), f32)
    ln_b = 0.02 * jax.random.normal(ks[5], (DEPTH, D_A), f32)
    sgu_w = jax.random.normal(ks[6], (DEPTH, SGU_GROUPS, SGU_BLOCK, SGU_BLOCK), f32) * (SGU_BLOCK ** -0.5)
    sgu_b = 1.0 + 0.02 * jax.random.normal(ks[7], (DEPTH, SGU_GROUPS, SGU_BLOCK), f32)
    lam_q1 = 0.1 * jax.random.normal(ks[8], (DEPTH, DA_HD), f32)
    lam_k1 = 0.1 * jax.random.normal(ks[9], (DEPTH, DA_HD), f32)
    lam_q2 = 0.1 * jax.random.normal(ks[10], (DEPTH, DA_HD), f32)
    lam_k2 = 0.1 * jax.random.normal(ks[11], (DEPTH, DA_HD), f32)
    subln_g = 1.0 + 0.02 * jax.random.normal(ks[12], (DEPTH, 2 * DA_HD), f32)
    w_out = jax.random.normal(ks[13], (DEPTH, D_MIX, D_MODEL), f32) * (0.5 * D_MIX ** -0.5)
    final_g = 1.0 + 0.02 * jax.random.normal(ks[14], (D_MODEL,), f32)
    return {"x": x, "positions": positions, "norm_g": norm_g, "w_in": w_in,
            "ln_g": ln_g, "ln_b": ln_b, "sgu_w": sgu_w, "sgu_b": sgu_b,
            "lam_q1": lam_q1, "lam_k1": lam_k1, "lam_q2": lam_q2, "lam_k2": lam_k2,
            "subln_g": subln_g, "w_out": w_out, "final_g": final_g}


def reference(x, positions, norm_g, w_in, ln_g, ln_b, sgu_w, sgu_b,
              lam_q1, lam_k1, lam_q2, lam_k2, subln_g, w_out, final_g):
    B, S, _ = x.shape
    cos, sin = rope_tables(positions)
    cuts = list(np.cumsum(SPLITS)[:-1])
    for l in range(DEPTH):
        h = rmsnorm(x, norm_g[l])
        z = jnp.einsum('bsd,de->bse', h, w_in[l])
        u_a, v_a, g_a, q, k, v_b, g_b = jnp.split(z, cuts, axis=-1)

        u_a = jax.nn.gelu(u_a, approximate=False)
        v_a = layernorm(jax.nn.gelu(v_a, approximate=False), ln_g[l], ln_b[l])
        y_a = u_a * sgu_mix(v_a, sgu_w[l], sgu_b[l])
        y_a = y_a * jax.nn.silu(g_a)

        q = partial_rope(q.reshape(B, S, DA_HEADS, 2, DA_HD), cos, sin)
        k = partial_rope(k.reshape(B, S, DA_HEADS, 2, DA_HD), cos, sin)
        v_b = v_b.reshape(B, S, DA_HEADS, 2 * DA_HD)
        lam_init = lambda_init_fn(l)
        lam = (jnp.exp(jnp.sum(lam_q1[l].astype(jnp.float32) * lam_k1[l].astype(jnp.float32)))
               - jnp.exp(jnp.sum(lam_q2[l].astype(jnp.float32) * lam_k2[l].astype(jnp.float32)))
               + lam_init)
        o = diff_attention(q, k, v_b, lam)
        o = rmsnorm(o, subln_g[l]) * jnp.asarray(1.0 - lam_init, o.dtype)
        y_b = o.reshape(B, S, D_B) * jax.nn.silu(g_b)

        y = jnp.concatenate([y_a, y_b], axis=-1)
        x = x + jnp.einsum('bse,ed->bsd', y, w_out[l])
    return rmsnorm(x, final_g)
```

```python
import functools
import math

import jax
import jax.numpy as jnp
from jax import lax
from jax.experimental import pallas as pl
from jax.experimental.pallas import tpu as pltpu

D_MODEL = 4096
DEPTH = 2
CHUNK = 64
D_A = D_MODEL // 2
D_B = D_MODEL - D_A
SGU_BLOCK = 128
SGU_GROUPS = 8
SGU_GW = D_A // SGU_GROUPS
DA_HEADS = 8
DA_HD = D_B // DA_HEADS // 2
ROT_DIM = DA_HD // 4
ROPE_THETA = 500000.0
EPS = 1e-5

LANES = 128
VMEM_LIMIT = 56 * 1024 * 1024
NEG_BIG = -1e30

BM = 1024
BN = 1024
NORM_ROWS = 256
SGU_ROWS = 256
TQ = 512
TK = 512

F32 = jnp.float32
BF16 = jnp.bfloat16


def _params(*sem):
    return pltpu.CompilerParams(dimension_semantics=sem, vmem_limit_bytes=VMEM_LIMIT)


def _rmsnorm_kernel(x_ref, g_ref, o_ref):
    x = x_ref[...]
    r = lax.rsqrt(jnp.mean(x * x, axis=-1, keepdims=True) + EPS)
    o_ref[...] = (x * r * g_ref[...]).astype(o_ref.dtype)


def _rmsnorm(x, g, out_dtype, name):
    t, d = x.shape
    return pl.pallas_call(
        _rmsnorm_kernel,
        out_shape=jax.ShapeDtypeStruct((t, d), out_dtype),
        grid=(t // NORM_ROWS,),
        in_specs=[pl.BlockSpec((NORM_ROWS, d), lambda i: (i, 0)),
                  pl.BlockSpec((1, d), lambda i: (0, 0))],
        out_specs=pl.BlockSpec((NORM_ROWS, d), lambda i: (i, 0)),
        compiler_params=_params("arbitrary"),
        name=name,
    )(x, g.reshape(1, d))


def _rope_table_kernel(pos_ref, freq_ref, c_ref, s1_ref, s2_ref):
    ang = pos_ref[...] * freq_ref[...]
    cos = jnp.cos(ang)
    sin = jnp.sin(ang)
    lane = lax.broadcasted_iota(jnp.int32, ang.shape, 1)
    half = ROT_DIM // 2
    c_ref[...] = jnp.where(lane < ROT_DIM, cos, 1.0)
    s1_ref[...] = jnp.where(lane < half, -sin, 0.0)
    s2_ref[...] = jnp.where((lane >= half) & (lane < ROT_DIM), sin, 0.0)


def _rope_tables(positions):
    t = positions.size
    rows = 2048
    half = ROT_DIM // 2
    inv_freq = ROPE_THETA ** (-jnp.arange(0, ROT_DIM, 2, dtype=F32) / ROT_DIM)
    freq_row = jnp.zeros((1, LANES), F32).at[0, :ROT_DIM].set(jnp.tile(inv_freq, 2))
    pos = positions.astype(F32).reshape(t, 1)
    tab = jax.ShapeDtypeStruct((t, LANES), F32)
    spec = pl.BlockSpec((rows, LANES), lambda i: (i, 0))
    return pl.pallas_call(
        _rope_table_kernel,
        out_shape=(tab, tab, tab),
        grid=(t // rows,),
        in_specs=[pl.BlockSpec((rows, 1), lambda i: (i, 0)),
                  pl.BlockSpec((1, LANES), lambda i: (0, 0))],
        out_specs=(spec, spec, spec),
        compiler_params=_params("arbitrary"),
        name="rope_tables",
    )(pos, freq_row)


def _gelu(a):
    return 0.5 * a * (1.0 + lax.erf(a * (1.0 / math.sqrt(2.0))))


def _silu(a):
    return a / (1.0 + jnp.exp(-a))


def _inproj_elementwise_kernel(fn, x_ref, w_ref, o_ref):
    acc = jnp.dot(x_ref[...], w_ref[...], preferred_element_type=F32)
    o_ref[...] = fn(acc).astype(o_ref.dtype)


def _inproj_rope_kernel(q_blocks, x_ref, w_ref, c_ref, s1_ref, s2_ref, o_ref):
    acc = jnp.dot(x_ref[...], w_ref[...], preferred_element_type=F32)
    scale = jnp.where(pl.program_id(0) < q_blocks, DA_HD ** -0.5, 1.0).astype(F32)
    c = c_ref[...]
    s1 = s1_ref[...]
    s2 = s2_ref[...]
    half = ROT_DIM // 2
    for g in range(BN // LANES):
        a = acc[:, g * LANES:(g + 1) * LANES]
        up = pltpu.roll(a, LANES - half, 1)
        down = pltpu.roll(a, half, 1)
        out = (a * c + up * s1 + down * s2) * scale
        o_ref[:, g * LANES:(g + 1) * LANES] = out.astype(o_ref.dtype)


def _inproj(h, w, col_block, n_blocks, body, extra=(), name=None):
    t, d = h.shape
    extra_specs = [pl.BlockSpec((BM, LANES), lambda j, i: (i, 0)) for _ in extra]
    return pl.pallas_call(
        body,
        out_shape=jax.ShapeDtypeStruct((t, n_blocks * BN), BF16),
        grid=(n_blocks, t // BM),
        in_specs=[pl.BlockSpec((BM, d), lambda j, i: (i, 0)),
                  pl.BlockSpec((d, BN), lambda j, i: (0, col_block(j)))] + extra_specs,
        out_specs=pl.BlockSpec((BM, BN), lambda j, i: (i, j)),
        compiler_params=_params("arbitrary", "arbitrary"),
        name=name,
    )(h, w, *extra)


def _sgu_kernel(gu_ref, gv_ref, sg_ref, lng_ref, lnb_ref, w_ref, b_ref, o_ref):
    v = gv_ref[...].astype(F32)
    mu = jnp.mean(v, axis=-1, keepdims=True)
    xc = v - mu
    var = jnp.mean(xc * xc, axis=-1, keepdims=True)
    vn = (xc * lax.rsqrt(var + EPS) * lng_ref[...] + lnb_ref[...]).astype(BF16)
    row = lax.broadcasted_iota(jnp.int32, (SGU_BLOCK, SGU_BLOCK), 0)
    col = lax.broadcasted_iota(jnp.int32, (SGU_BLOCK, SGU_BLOCK), 1)
    mask = (col // CHUNK) <= (row // CHUNK)
    for g in range(SGU_GROUPS):
        wg = jnp.where(mask, w_ref[g], 0.0).astype(BF16)
        bg = b_ref[:, g:g + 1]
        cs = slice(g * SGU_GW, (g + 1) * SGU_GW)
        for r in range(SGU_ROWS // SGU_BLOCK):
            rs = slice(r * SGU_BLOCK, (r + 1) * SGU_BLOCK)
            mix = jnp.dot(wg, vn[rs, cs], preferred_element_type=F32) + bg
            y = gu_ref[rs, cs].astype(F32) * mix * sg_ref[rs, cs].astype(F32)
            o_ref[rs, cs] = y.astype(o_ref.dtype)


def _sgu(guv, sgate, ln_g, ln_b, sgu_w, sgu_b):
    t = guv.shape[0]
    return pl.pallas_call(
        _sgu_kernel,
        out_shape=jax.ShapeDtypeStruct((t, D_A), BF16),
        grid=(t // SGU_ROWS,),
        in_specs=[pl.BlockSpec((SGU_ROWS, D_A), lambda i: (i, 0)),
                  pl.BlockSpec((SGU_ROWS, D_A), lambda i: (i, 1)),
                  pl.BlockSpec((SGU_ROWS, D_A), lambda i: (i, 0)),
                  pl.BlockSpec((1, D_A), lambda i: (0, 0)),
                  pl.BlockSpec((1, D_A), lambda i: (0, 0)),
                  pl.BlockSpec((SGU_GROUPS, SGU_BLOCK, SGU_BLOCK), lambda i: (0, 0, 0)),
                  pl.BlockSpec((SGU_BLOCK, SGU_GROUPS), lambda i: (0, 0))],
        out_specs=pl.BlockSpec((SGU_ROWS, D_A), lambda i: (i, 0)),
        compiler_params=_params("arbitrary"),
        name="sgu",
    )(guv, guv, sgate, ln_g.reshape(1, D_A), ln_b.reshape(1, D_A), sgu_w, sgu_b.T)


def _attn_kernel(lam_init, q_ref, k_ref, v_ref, g_ref, lq1_ref, lk1_ref, lq2_ref, lk2_ref, sg_ref,
                 o_ref, acc_ref, m_ref, l_ref):
    qi = pl.program_id(2)

    m_ref[...] = jnp.full(m_ref.shape, NEG_BIG, F32)
    l_ref[...] = jnp.zeros(l_ref.shape, F32)
    acc_ref[...] = jnp.zeros(acc_ref.shape, F32)

    def step(kb, masked):
        ks = pl.ds(pl.multiple_of(kb * TK, TK), TK)
        v_blk = v_ref[ks, :]
        if masked:
            row = lax.broadcasted_iota(jnp.int32, (TQ, TK), 0)
            col = lax.broadcasted_iota(jnp.int32, (TQ, TK), 1)
            mask = (col // CHUNK) <= (row // CHUNK)
        for c in range(2):
            cs = slice(c * DA_HD, (c + 1) * DA_HD)
            s = lax.dot_general(q_ref[:, cs], k_ref[ks, cs], (((1,), (1,)), ((), ())),
                                preferred_element_type=F32)
            if masked:
                s = jnp.where(mask, s, NEG_BIG)
            m_prev = m_ref[c]
            m_new = jnp.maximum(m_prev, jnp.max(s, axis=-1, keepdims=True))
            alpha = jnp.exp(m_prev - m_new)
            p = jnp.exp(s - m_new)
            l_ref[c] = alpha * l_ref[c] + jnp.sum(p, axis=-1, keepdims=True)
            acc_ref[c] = alpha * acc_ref[c] + jnp.dot(p.astype(BF16), v_blk,
                                                      preferred_element_type=F32)
            m_ref[c] = m_new

    def body(kb, carry):
        step(kb, False)
        return carry

    lax.fori_loop(0, qi, body, 0)
    step(qi, True)

    lam = (jnp.exp(jnp.sum(lq1_ref[...] * lk1_ref[...], axis=-1, keepdims=True))
           - jnp.exp(jnp.sum(lq2_ref[...] * lk2_ref[...], axis=-1, keepdims=True))
           + lam_init)
    o = acc_ref[0] / l_ref[0] - lam * (acc_ref[1] / l_ref[1])
    r = lax.rsqrt(jnp.mean(o * o, axis=-1, keepdims=True) + EPS)
    y = (o * r * sg_ref[...]) * (1.0 - lam_init) * g_ref[...].astype(F32)
    o_ref[...] = y.astype(o_ref.dtype)


def _attention(qk, vb, gates, lq1, lk1, lq2, lk2, subln_g, lam_init, batch, seq):
    t = qk.shape[0]
    nq = seq // TQ
    hw = 2 * DA_HD
    vec = pl.BlockSpec((1, DA_HD), lambda b, h, i: (0, 0))
    return pl.pallas_call(
        functools.partial(_attn_kernel, lam_init),
        out_shape=jax.ShapeDtypeStruct((t, D_B), BF16),
        grid=(batch, DA_HEADS, nq),
        in_specs=[pl.BlockSpec((TQ, hw), lambda b, h, i: (b * nq + i, h)),
                  pl.BlockSpec((seq, hw), lambda b, h, i: (b, DA_HEADS + h)),
                  pl.BlockSpec((seq, hw), lambda b, h, i: (b, h)),
                  pl.BlockSpec((TQ, hw), lambda b, h, i: (b * nq + i, DA_HEADS + h)),
                  vec, vec, vec, vec,
                  pl.BlockSpec((1, hw), lambda b, h, i: (0, 0))],
        out_specs=pl.BlockSpec((TQ, hw), lambda b, h, i: (b * nq + i, h)),
        scratch_shapes=[pltpu.VMEM((2, TQ, hw), F32),
                        pltpu.VMEM((2, TQ, 1), F32),
                        pltpu.VMEM((2, TQ, 1), F32)],
        compiler_params=_params("arbitrary", "arbitrary", "arbitrary"),
        name="diff_attention",
    )(qk, qk, vb, gates, lq1.reshape(1, DA_HD), lk1.reshape(1, DA_HD),
      lq2.reshape(1, DA_HD), lk2.reshape(1, DA_HD), subln_g.reshape(1, hw))


def _outproj_kernel(ya_ref, yb_ref, wa_ref, wb_ref, x_ref, o_ref):
    acc = jnp.dot(ya_ref[...], wa_ref[...], preferred_element_type=F32)
    acc = acc + jnp.dot(yb_ref[...], wb_ref[...], preferred_element_type=F32)
    o_ref[...] = x_ref[...] + acc


def _outproj(ya, yb, w, x):
    t, d = x.shape
    return pl.pallas_call(
        _outproj_kernel,
        out_shape=jax.ShapeDtypeStruct((t, d), F32),
        grid=(d // BN, t // BM),
        in_specs=[pl.BlockSpec((BM, D_A), lambda j, i: (i, 0)),
                  pl.BlockSpec((BM, D_B), lambda j, i: (i, 0)),
                  pl.BlockSpec((D_A, BN), lambda j, i: (0, j)),
                  pl.BlockSpec((D_B, BN), lambda j, i: (1, j)),
                  pl.BlockSpec((BM, BN), lambda j, i: (i, j))],
        out_specs=pl.BlockSpec((BM, BN), lambda j, i: (i, j)),
        compiler_params=_params("arbitrary", "arbitrary"),
        name="outproj",
    )(ya, yb, w, w, x)


def kernel(x, positions, norm_g, w_in, ln_g, ln_b, sgu_w, sgu_b, lam_q1, lam_k1, lam_q2, lam_k2,
           subln_g, w_out, final_g):
    batch, seq, d = x.shape
    t = batch * seq
    x = x.reshape(t, d)
    rope_c, rope_s1, rope_s2 = _rope_tables(positions)
    nb = D_A // BN
    for l in range(DEPTH):
        w = w_in[l].astype(BF16)
        h = _rmsnorm(x, norm_g[l], BF16, "rmsnorm_in")
        guv = _inproj(h, w, lambda j: j, 2 * nb,
                      functools.partial(_inproj_elementwise_kernel, _gelu), name="inproj_gelu")
        gates = _inproj(h, w, lambda j: jnp.where(j < nb, 2 * nb + j, 5 * nb + j), 2 * nb,
                        functools.partial(_inproj_elementwise_kernel, _silu), name="inproj_silu")
        qk = _inproj(h, w, lambda j: 3 * nb + j, 2 * nb,
                     functools.partial(_inproj_rope_kernel, nb),
                     extra=(rope_c, rope_s1, rope_s2), name="inproj_rope")
        vb = _inproj(h, w, lambda j: 5 * nb + j, nb,
                     functools.partial(_inproj_elementwise_kernel, lambda a: a), name="inproj_v")
        ya = _sgu(guv, gates, ln_g[l], ln_b[l], sgu_w[l], sgu_b[l])
        lam_init = 0.8 - 0.6 * math.exp(-0.3 * l)
        yb = _attention(qk, vb, gates, lam_q1[l], lam_k1[l], lam_q2[l], lam_k2[l], subln_g[l],
                        lam_init, batch, seq)
        x = _outproj(ya, yb, w_out[l].astype(BF16), x)
    out = _rmsnorm(x, final_g, F32, "rmsnorm_out")
    return out.reshape(batch, seq, d)
```

```python
import functools
import math

import jax
import jax.numpy as jnp
from jax import lax
from jax.experimental import pallas as pl
from jax.experimental.pallas import tpu as pltpu

D_MODEL = 4096
DEPTH = 2
CHUNK = 64
D_A = D_MODEL // 2
D_B = D_MODEL - D_A
SGU_BLOCK = 128
SGU_GROUPS = 8
SGU_GW = D_A // SGU_GROUPS
DA_HEADS = 8
DA_HD = D_B // DA_HEADS // 2
ROT_DIM = DA_HD // 4
ROPE_THETA = 500000.0
EPS = 1e-5

LANES = 128
VMEM_LIMIT = 56 * 1024 * 1024
NEG_BIG = -1e30

BM = 1024
BN = 1024
NORM_ROWS = 256
SGU_ROWS = 256
TQ = 1024
TK = TQ // 2
ATT_ROWS = 256

F32 = jnp.float32
BF16 = jnp.bfloat16


def _params(*sem):
    return pltpu.CompilerParams(dimension_semantics=sem, vmem_limit_bytes=VMEM_LIMIT)


def _rmsnorm_kernel(x_ref, g_ref, o_ref):
    x = x_ref[...]
    r = lax.rsqrt(jnp.mean(x * x, axis=-1, keepdims=True) + EPS)
    o_ref[...] = (x * r * g_ref[...]).astype(o_ref.dtype)


def _rmsnorm(x, g, out_dtype, name):
    t, d = x.shape
    return pl.pallas_call(
        _rmsnorm_kernel,
        out_shape=jax.ShapeDtypeStruct((t, d), out_dtype),
        grid=(t // NORM_ROWS,),
        in_specs=[pl.BlockSpec((NORM_ROWS, d), lambda i: (i, 0)),
                  pl.BlockSpec((1, d), lambda i: (0, 0))],
        out_specs=pl.BlockSpec((NORM_ROWS, d), lambda i: (i, 0)),
        compiler_params=_params("arbitrary"),
        name=name,
    )(x, g.reshape(1, d))


def _rope_table_kernel(pos_ref, freq_ref, c_ref, s1_ref, s2_ref):
    ang = pos_ref[...] * freq_ref[...]
    cos = jnp.cos(ang)
    sin = jnp.sin(ang)
    lane = lax.broadcasted_iota(jnp.int32, ang.shape, 1)
    half = ROT_DIM // 2
    c_ref[...] = jnp.where(lane < ROT_DIM, cos, 1.0)
    s1_ref[...] = jnp.where(lane < half, -sin, 0.0)
    s2_ref[...] = jnp.where((lane >= half) & (lane < ROT_DIM), sin, 0.0)


def _rope_tables(positions):
    t = positions.size
    rows = 2048
    half = ROT_DIM // 2
    inv_freq = ROPE_THETA ** (-jnp.arange(0, ROT_DIM, 2, dtype=F32) / ROT_DIM)
    freq_row = jnp.zeros((1, LANES), F32).at[0, :ROT_DIM].set(jnp.tile(inv_freq, 2))
    pos = positions.astype(F32).reshape(t, 1)
    tab = jax.ShapeDtypeStruct((t, LANES), F32)
    spec = pl.BlockSpec((rows, LANES), lambda i: (i, 0))
    return pl.pallas_call(
        _rope_table_kernel,
        out_shape=(tab, tab, tab),
        grid=(t // rows,),
        in_specs=[pl.BlockSpec((rows, 1), lambda i: (i, 0)),
                  pl.BlockSpec((1, LANES), lambda i: (0, 0))],
        out_specs=(spec, spec, spec),
        compiler_params=_params("arbitrary"),
        name="rope_tables",
    )(pos, freq_row)


def _gelu(a):
    return 0.5 * a * (1.0 + lax.erf(a * (1.0 / math.sqrt(2.0))))


def _silu(a):
    return a / (1.0 + jnp.exp(-a))


def _inproj_elementwise_kernel(fn, x_ref, w_ref, o_ref):
    acc = jnp.dot(x_ref[...], w_ref[...], preferred_element_type=F32)
    o_ref[...] = fn(acc).astype(o_ref.dtype)


def _inproj_rope_kernel(scale, transposed, x_ref, w_ref, c_ref, s1_ref, s2_ref, o_ref):
    acc = jnp.dot(x_ref[...], w_ref[...], preferred_element_type=F32)
    c = c_ref[...] * scale
    s1 = s1_ref[...] * scale
    s2 = s2_ref[...] * scale
    half = ROT_DIM // 2
    for g in range(BN // LANES):
        cols = slice(g * LANES, (g + 1) * LANES)
        a = acc[:, cols]
        up = pltpu.roll(a, LANES - half, 1)
        down = pltpu.roll(a, half, 1)
        out = a * c + up * s1 + down * s2
        if transposed:
            for kb in range(BM // TK):
                o_ref[kb, cols, :] = out[kb * TK:(kb + 1) * TK, :].T.astype(o_ref.dtype)
        else:
            o_ref[:, cols] = out.astype(o_ref.dtype)


def _inproj(h, w, col_block, n_blocks, body, extra=(), transposed=False, name=None):
    t, d = h.shape
    extra_specs = [pl.BlockSpec((BM, LANES), lambda j, i: (i, 0)) for _ in extra]
    if transposed:
        out_shape = jax.ShapeDtypeStruct((t // TK, n_blocks * BN, TK), BF16)
        out_spec = pl.BlockSpec((BM // TK, BN, TK), lambda j, i: (i, j, 0))
    else:
        out_shape = jax.ShapeDtypeStruct((t, n_blocks * BN), BF16)
        out_spec = pl.BlockSpec((BM, BN), lambda j, i: (i, j))
    return pl.pallas_call(
        body,
        out_shape=out_shape,
        grid=(n_blocks, t // BM),
        in_specs=[pl.BlockSpec((BM, d), lambda j, i: (i, 0)),
                  pl.BlockSpec((d, BN), lambda j, i: (0, col_block(j)))] + extra_specs,
        out_specs=out_spec,
        compiler_params=_params("arbitrary", "arbitrary"),
        name=name,
    )(h, w, *extra)


def _sgu_kernel(gu_ref, gv_ref, sg_ref, lng_ref, lnb_ref, w_ref, b_ref, o_ref):
    v = gv_ref[...].astype(F32)
    mu = jnp.mean(v, axis=-1, keepdims=True)
    xc = v - mu
    var = jnp.mean(xc * xc, axis=-1, keepdims=True)
    vn = (xc * lax.rsqrt(var + EPS) * lng_ref[...] + lnb_ref[...]).astype(BF16)
    row = lax.broadcasted_iota(jnp.int32, (SGU_BLOCK, SGU_BLOCK), 0)
    col = lax.broadcasted_iota(jnp.int32, (SGU_BLOCK, SGU_BLOCK), 1)
    mask = (col // CHUNK) <= (row // CHUNK)
    for g in range(SGU_GROUPS):
        wg = jnp.where(mask, w_ref[g], 0.0).astype(BF16)
        bg = b_ref[:, g:g + 1]
        cs = slice(g * SGU_GW, (g + 1) * SGU_GW)
        for r in range(SGU_ROWS // SGU_BLOCK):
            rs = slice(r * SGU_BLOCK, (r + 1) * SGU_BLOCK)
            mix = jnp.dot(wg, vn[rs, cs], preferred_element_type=F32) + bg
            y = gu_ref[rs, cs].astype(F32) * mix * sg_ref[rs, cs].astype(F32)
            o_ref[rs, cs] = y.astype(o_ref.dtype)


def _sgu(guv, sgate, ln_g, ln_b, sgu_w, sgu_b):
    t = guv.shape[0]
    return pl.pallas_call(
        _sgu_kernel,
        out_shape=jax.ShapeDtypeStruct((t, D_A), BF16),
        grid=(t // SGU_ROWS,),
        in_specs=[pl.BlockSpec((SGU_ROWS, D_A), lambda i: (i, 0)),
                  pl.BlockSpec((SGU_ROWS, D_A), lambda i: (i, 1)),
                  pl.BlockSpec((SGU_ROWS, D_A), lambda i: (i, 0)),
                  pl.BlockSpec((1, D_A), lambda i: (0, 0)),
                  pl.BlockSpec((1, D_A), lambda i: (0, 0)),
                  pl.BlockSpec((SGU_GROUPS, SGU_BLOCK, SGU_BLOCK), lambda i: (0, 0, 0)),
                  pl.BlockSpec((SGU_BLOCK, SGU_GROUPS), lambda i: (0, 0))],
        out_specs=pl.BlockSpec((SGU_ROWS, D_A), lambda i: (i, 0)),
        compiler_params=_params("arbitrary"),
        name="sgu",
    )(guv, guv, sgate, ln_g.reshape(1, D_A), ln_b.reshape(1, D_A), sgu_w, sgu_b.T)


def _attn_kernel(lam_init, q_ref, kt_ref, v_ref, g_ref, lq1_ref, lk1_ref, lq2_ref, lk2_ref, sg_ref,
                 o_ref, acc_ref, m_ref, l_ref):
    qi = pl.program_id(2)

    m_ref[...] = jnp.full(m_ref.shape, NEG_BIG, F32)
    l_ref[...] = jnp.zeros(l_ref.shape, F32)
    acc_ref[...] = jnp.zeros(acc_ref.shape, F32)

    def block(kb, col0):
        v_blk = v_ref[pl.ds(pl.multiple_of(kb * TK, TK), TK), :]
        for r0 in range(0, TQ, ATT_ROWS):
            if col0 is not None and r0 + ATT_ROWS <= col0:
                continue
            rows = slice(r0, r0 + ATT_ROWS)
            masked = col0 is not None and col0 + TK > r0
            if masked:
                row = lax.broadcasted_iota(jnp.int32, (ATT_ROWS, TK), 0) + r0
                col = lax.broadcasted_iota(jnp.int32, (ATT_ROWS, TK), 1) + col0
                mask = (col // CHUNK) <= (row // CHUNK)
            for c in range(2):
                cs = slice(c * DA_HD, (c + 1) * DA_HD)
                s = jnp.dot(q_ref[rows, cs], kt_ref[kb, cs, :], preferred_element_type=F32)
                if masked:
                    s = jnp.where(mask, s, NEG_BIG)
                m_prev = m_ref[c, rows, :]
                m_new = jnp.maximum(m_prev, jnp.max(s, axis=-1, keepdims=True))
                alpha = jnp.exp2(m_prev - m_new)
                ps = [jnp.exp2(s[:, j * LANES:(j + 1) * LANES] - m_new) for j in range(TK // LANES)]
                l_ref[c, rows, :] = alpha * l_ref[c, rows, :] + functools.reduce(jnp.add, ps)
                m_ref[c, rows, :] = m_new
                p = jnp.concatenate(ps, axis=-1).astype(BF16)
                pv = jnp.dot(p, v_blk, preferred_element_type=F32)
                scale = jnp.concatenate([alpha] * (2 * DA_HD // LANES), axis=-1)
                acc_ref[c, rows, :] = scale * acc_ref[c, rows, :] + pv

    def body(pair, carry):
        block(2 * pair, None)
        block(2 * pair + 1, None)
        return carry

    lax.fori_loop(0, qi, body, 0)
    block(2 * qi, 0)
    block(2 * qi + 1, TK)

    lam = (jnp.exp(jnp.sum(lq1_ref[...] * lk1_ref[...], axis=-1, keepdims=True))
           - jnp.exp(jnp.sum(lq2_ref[...] * lk2_ref[...], axis=-1, keepdims=True))
           + lam_init)
    l0 = jnp.sum(l_ref[0], axis=-1, keepdims=True)
    l1 = jnp.sum(l_ref[1], axis=-1, keepdims=True)
    o = acc_ref[0] / l0 - lam * (acc_ref[1] / l1)
    r = lax.rsqrt(jnp.mean(o * o, axis=-1, keepdims=True) + EPS)
    y = (o * r * sg_ref[...]) * (1.0 - lam_init) * g_ref[...].astype(F32)
    o_ref[...] = y.astype(o_ref.dtype)


def _attention(q, kt, vb, gates, lq1, lk1, lq2, lk2, subln_g, lam_init, batch, seq):
    t = q.shape[0]
    nq = seq // TQ
    nk = seq // TK
    hw = 2 * DA_HD
    vec = pl.BlockSpec((1, DA_HD), lambda b, h, i: (0, 0))
    return pl.pallas_call(
        functools.partial(_attn_kernel, lam_init),
        out_shape=jax.ShapeDtypeStruct((t, D_B), BF16),
        grid=(batch, DA_HEADS, nq),
        in_specs=[pl.BlockSpec((TQ, hw), lambda b, h, i: (b * nq + i, h)),
                  pl.BlockSpec((nk, hw, TK), lambda b, h, i: (b, h, 0)),
                  pl.BlockSpec((seq, hw), lambda b, h, i: (b, h)),
                  pl.BlockSpec((TQ, hw), lambda b, h, i: (b * nq + i, DA_HEADS + h)),
                  vec, vec, vec, vec,
                  pl.BlockSpec((1, hw), lambda b, h, i: (0, 0))],
        out_specs=pl.BlockSpec((TQ, hw), lambda b, h, i: (b * nq + i, h)),
        scratch_shapes=[pltpu.VMEM((2, TQ, hw), F32),
                        pltpu.VMEM((2, TQ, LANES), F32),
                        pltpu.VMEM((2, TQ, LANES), F32)],
        compiler_params=_params("arbitrary", "arbitrary", "arbitrary"),
        name="diff_attention",
    )(q, kt, vb, gates, lq1.reshape(1, DA_HD), lk1.reshape(1, DA_HD),
      lq2.reshape(1, DA_HD), lk2.reshape(1, DA_HD), subln_g.reshape(1, hw))


def _outproj_kernel(ya_ref, yb_ref, wa_ref, wb_ref, x_ref, o_ref):
    acc = jnp.dot(ya_ref[...], wa_ref[...], preferred_element_type=F32)
    acc = acc + jnp.dot(yb_ref[...], wb_ref[...], preferred_element_type=F32)
    o_ref[...] = x_ref[...] + acc


def _outproj(ya, yb, w, x):
    t, d = x.shape
    return pl.pallas_call(
        _outproj_kernel,
        out_shape=jax.ShapeDtypeStruct((t, d), F32),
        grid=(d // BN, t // BM),
        in_specs=[pl.BlockSpec((BM, D_A), lambda j, i: (i, 0)),
                  pl.BlockSpec((BM, D_B), lambda j, i: (i, 0)),
                  pl.BlockSpec((D_A, BN), lambda j, i: (0, j)),
                  pl.BlockSpec((D_B, BN), lambda j, i: (1, j)),
                  pl.BlockSpec((BM, BN), lambda j, i: (i, j))],
        out_specs=pl.BlockSpec((BM, BN), lambda j, i: (i, j)),
        compiler_params=_params("arbitrary", "arbitrary"),
        name="outproj",
    )(ya, yb, w, w, x)


def kernel(x, positions, norm_g, w_in, ln_g, ln_b, sgu_w, sgu_b, lam_q1, lam_k1, lam_q2, lam_k2,
           subln_g, w_out, final_g):
    batch, seq, d = x.shape
    t = batch * seq
    x = x.reshape(t, d)
    rope_c, rope_s1, rope_s2 = _rope_tables(positions)
    nb = D_A // BN
    for l in range(DEPTH):
        w = w_in[l].astype(BF16)
        h = _rmsnorm(x, norm_g[l], BF16, "rmsnorm_in")
        guv = _inproj(h, w, lambda j: j, 2 * nb,
                      functools.partial(_inproj_elementwise_kernel, _gelu), name="inproj_gelu")
        gates = _inproj(h, w, lambda j: jnp.where(j < nb, 2 * nb + j, 5 * nb + j), 2 * nb,
                        functools.partial(_inproj_elementwise_kernel, _silu), name="inproj_silu")
        q = _inproj(h, w, lambda j: 3 * nb + j, nb,
                    functools.partial(_inproj_rope_kernel, DA_HD ** -0.5 * math.log2(math.e), False),
                    extra=(rope_c, rope_s1, rope_s2), name="inproj_q")
        kt = _inproj(h, w, lambda j: 4 * nb + j, nb,
                     functools.partial(_inproj_rope_kernel, 1.0, True),
                     extra=(rope_c, rope_s1, rope_s2), transposed=True, name="inproj_k")
        vb = _inproj(h, w, lambda j: 5 * nb + j, nb,
                     functools.partial(_inproj_elementwise_kernel, lambda a: a), name="inproj_v")
        ya = _sgu(guv, gates, ln_g[l], ln_b[l], sgu_w[l], sgu_b[l])
        lam_init = 0.8 - 0.6 * math.exp(-0.3 * l)
        yb = _attention(q, kt, vb, gates, lam_q1[l], lam_k1[l], lam_q2[l], lam_k2[l], subln_g[l],
                        lam_init, batch, seq)
        x = _outproj(ya, yb, w_out[l].astype(BF16), x)
    out = _rmsnorm(x, final_g, F32, "rmsnorm_out")
    return out.reshape(batch, seq, d)
```

```python
import functools
import math

import jax
import jax.numpy as jnp
from jax import lax
from jax.experimental import pallas as pl
from jax.experimental.pallas import tpu as pltpu

D_MODEL = 4096
DEPTH = 2
CHUNK = 64
D_A = D_MODEL // 2
D_B = D_MODEL - D_A
SGU_BLOCK = 128
SGU_GROUPS = 8
SGU_GW = D_A // SGU_GROUPS
DA_HEADS = 8
DA_HD = D_B // DA_HEADS // 2
ROT_DIM = DA_HD // 4
ROPE_THETA = 500000.0
EPS = 1e-5

LANES = 128
VMEM_LIMIT = 56 * 1024 * 1024
NEG_BIG = -1e30

BM = 1024
BN = 1024
NORM_ROWS = 256
SGU_ROWS = 256
TQ = 1024
TK = TQ // 2
ATT_ROWS = 128
PV_ROWS = 256

F32 = jnp.float32
BF16 = jnp.bfloat16


def _params(*sem):
    return pltpu.CompilerParams(dimension_semantics=sem, vmem_limit_bytes=VMEM_LIMIT)


def _rmsnorm_kernel(x_ref, g_ref, o_ref):
    x = x_ref[...]
    r = lax.rsqrt(jnp.mean(x * x, axis=-1, keepdims=True) + EPS)
    o_ref[...] = (x * r * g_ref[...]).astype(o_ref.dtype)


def _rmsnorm(x, g, out_dtype, name):
    t, d = x.shape
    return pl.pallas_call(
        _rmsnorm_kernel,
        out_shape=jax.ShapeDtypeStruct((t, d), out_dtype),
        grid=(t // NORM_ROWS,),
        in_specs=[pl.BlockSpec((NORM_ROWS, d), lambda i: (i, 0)),
                  pl.BlockSpec((1, d), lambda i: (0, 0))],
        out_specs=pl.BlockSpec((NORM_ROWS, d), lambda i: (i, 0)),
        compiler_params=_params("arbitrary"),
        name=name,
    )(x, g.reshape(1, d))


def _rope_table_kernel(pos_ref, freq_ref, c_ref, s1_ref, s2_ref):
    ang = pos_ref[...] * freq_ref[...]
    cos = jnp.cos(ang)
    sin = jnp.sin(ang)
    lane = lax.broadcasted_iota(jnp.int32, ang.shape, 1)
    half = ROT_DIM // 2
    c_ref[...] = jnp.where(lane < ROT_DIM, cos, 1.0)
    s1_ref[...] = jnp.where(lane < half, -sin, 0.0)
    s2_ref[...] = jnp.where((lane >= half) & (lane < ROT_DIM), sin, 0.0)


def _rope_tables(positions):
    t = positions.size
    rows = 2048
    inv_freq = ROPE_THETA ** (-jnp.arange(0, ROT_DIM, 2, dtype=F32) / ROT_DIM)
    freq_row = jnp.zeros((1, LANES), F32).at[0, :ROT_DIM].set(jnp.tile(inv_freq, 2))
    pos = positions.astype(F32).reshape(t, 1)
    tab = jax.ShapeDtypeStruct((t, LANES), F32)
    spec = pl.BlockSpec((rows, LANES), lambda i: (i, 0))
    return pl.pallas_call(
        _rope_table_kernel,
        out_shape=(tab, tab, tab),
        grid=(t // rows,),
        in_specs=[pl.BlockSpec((rows, 1), lambda i: (i, 0)),
                  pl.BlockSpec((1, LANES), lambda i: (0, 0))],
        out_specs=(spec, spec, spec),
        compiler_params=_params("arbitrary"),
        name="rope_tables",
    )(pos, freq_row)


def _gelu(a):
    return 0.5 * a * (1.0 + lax.erf(a * (1.0 / math.sqrt(2.0))))


def _silu(a):
    return a / (1.0 + jnp.exp(-a))


def _identity(a):
    return a


def _inproj_elementwise_kernel(fn, x_ref, w_ref, o_ref):
    acc = jnp.dot(x_ref[...], w_ref[...], preferred_element_type=F32)
    o_ref[...] = fn(acc).astype(o_ref.dtype)


def _inproj_rope_kernel(scale, transposed, x_ref, w_ref, c_ref, s1_ref, s2_ref, o_ref):
    acc = jnp.dot(x_ref[...], w_ref[...], preferred_element_type=F32)
    c = c_ref[...] * scale
    s1 = s1_ref[...] * scale
    s2 = s2_ref[...] * scale
    half = ROT_DIM // 2
    for g in range(BN // LANES):
        cols = slice(g * LANES, (g + 1) * LANES)
        a = acc[:, cols]
        up = pltpu.roll(a, LANES - half, 1)
        down = pltpu.roll(a, half, 1)
        out = a * c + up * s1 + down * s2
        if transposed:
            for kb in range(BM // TK):
                o_ref[kb, cols, :] = out[kb * TK:(kb + 1) * TK, :].T.astype(o_ref.dtype)
        else:
            o_ref[:, cols] = out.astype(o_ref.dtype)


def _inproj(h, w, layer, col_block, n_blocks, body, extra=(), transposed=False, name=None):
    t, d = h.shape
    extra_specs = [pl.BlockSpec((BM, LANES), lambda j, i: (i, 0)) for _ in extra]
    if transposed:
        out_shape = jax.ShapeDtypeStruct((t // TK, n_blocks * BN, TK), BF16)
        out_spec = pl.BlockSpec((BM // TK, BN, TK), lambda j, i: (i, j, 0))
    else:
        out_shape = jax.ShapeDtypeStruct((t, n_blocks * BN), BF16)
        out_spec = pl.BlockSpec((BM, BN), lambda j, i: (i, j))
    return pl.pallas_call(
        body,
        out_shape=out_shape,
        grid=(n_blocks, t // BM),
        in_specs=[pl.BlockSpec((BM, d), lambda j, i: (i, 0)),
                  pl.BlockSpec((None, d, BN), lambda j, i: (layer, 0, col_block(j)))] + extra_specs,
        out_specs=out_spec,
        compiler_params=_params("arbitrary", "arbitrary"),
        name=name,
    )(h, w, *extra)


def _sgu_kernel(gu_ref, gv_ref, sg_ref, lng_ref, lnb_ref, w_ref, b_ref, o_ref):
    v = gv_ref[...].astype(F32)
    mu = jnp.mean(v, axis=-1, keepdims=True)
    xc = v - mu
    var = jnp.mean(xc * xc, axis=-1, keepdims=True)
    vn = (xc * lax.rsqrt(var + EPS) * lng_ref[...] + lnb_ref[...]).astype(BF16)
    row = lax.broadcasted_iota(jnp.int32, (SGU_BLOCK, SGU_BLOCK), 0)
    col = lax.broadcasted_iota(jnp.int32, (SGU_BLOCK, SGU_BLOCK), 1)
    mask = (col // CHUNK) <= (row // CHUNK)
    for g in range(SGU_GROUPS):
        wg = jnp.where(mask, w_ref[g], 0.0).astype(BF16)
        bg = b_ref[:, g:g + 1]
        cs = slice(g * SGU_GW, (g + 1) * SGU_GW)
        for r in range(SGU_ROWS // SGU_BLOCK):
            rs = slice(r * SGU_BLOCK, (r + 1) * SGU_BLOCK)
            mix = jnp.dot(wg, vn[rs, cs], preferred_element_type=F32) + bg
            y = gu_ref[rs, cs].astype(F32) * mix * sg_ref[rs, cs].astype(F32)
            o_ref[rs, cs] = y.astype(o_ref.dtype)


def _sgu(guv, sgate, ln_g, ln_b, sgu_w, sgu_b):
    t = guv.shape[0]
    return pl.pallas_call(
        _sgu_kernel,
        out_shape=jax.ShapeDtypeStruct((t, D_A), BF16),
        grid=(t // SGU_ROWS,),
        in_specs=[pl.BlockSpec((SGU_ROWS, D_A), lambda i: (i, 0)),
                  pl.BlockSpec((SGU_ROWS, D_A), lambda i: (i, 1)),
                  pl.BlockSpec((SGU_ROWS, D_A), lambda i: (i, 0)),
                  pl.BlockSpec((1, D_A), lambda i: (0, 0)),
                  pl.BlockSpec((1, D_A), lambda i: (0, 0)),
                  pl.BlockSpec((SGU_GROUPS, SGU_BLOCK, SGU_BLOCK), lambda i: (0, 0, 0)),
                  pl.BlockSpec((SGU_BLOCK, SGU_GROUPS), lambda i: (0, 0))],
        out_specs=pl.BlockSpec((SGU_ROWS, D_A), lambda i: (i, 0)),
        compiler_params=_params("arbitrary"),
        name="sgu",
    )(guv, guv, sgate, ln_g.reshape(1, D_A), ln_b.reshape(1, D_A), sgu_w, sgu_b.T)


def _attn_kernel(lam_init, q_ref, kt_ref, v_ref, g_ref, lq1_ref, lk1_ref, lq2_ref, lk2_ref, sg_ref,
                 o_ref, p_ref, a_ref, acc_ref, m_ref, l_ref):
    qi = pl.program_id(2)

    m_ref[...] = jnp.full(m_ref.shape, NEG_BIG, F32)
    l_ref[...] = jnp.zeros(l_ref.shape, F32)
    acc_ref[...] = jnp.zeros(acc_ref.shape, F32)

    def probs(kb, slot, col0, diagonal_if_first=False):
        for r0 in range(0, TQ, ATT_ROWS):
            if col0 is not None and r0 + ATT_ROWS <= col0:
                continue
            rows = slice(r0, r0 + ATT_ROWS)
            masked = diagonal_if_first or (col0 is not None and col0 + TK > r0)
            if masked:
                row = lax.broadcasted_iota(jnp.int32, (ATT_ROWS, TK), 0) + r0
                col = lax.broadcasted_iota(jnp.int32, (ATT_ROWS, TK), 1) + (col0 or 0)
                row_chunk = row // CHUNK
                if diagonal_if_first:
                    row_chunk = row_chunk + jnp.where(qi == 0, 0, TQ)
                mask = (col // CHUNK) <= row_chunk
            for c in range(2):
                cs = slice(c * DA_HD, (c + 1) * DA_HD)
                s = jnp.dot(q_ref[rows, cs], kt_ref[kb, cs, :], preferred_element_type=F32)
                if masked:
                    s = jnp.where(mask, s, NEG_BIG)
                m_prev = m_ref[c, rows, :]
                m_new = jnp.maximum(m_prev, jnp.max(s, axis=-1, keepdims=True))
                alpha = jnp.exp2(m_prev - m_new)
                ps = [jnp.exp2(s[:, j * LANES:(j + 1) * LANES] - m_new) for j in range(TK // LANES)]
                l_ref[c, rows, :] = alpha * l_ref[c, rows, :] + functools.reduce(jnp.add, ps)
                m_ref[c, rows, :] = m_new
                a_ref[slot, c, rows, :] = alpha
                p_ref[slot, c, rows, :] = jnp.concatenate(ps, axis=-1).astype(BF16)

    def values(kb, slot, first_row=0):
        v_blk = v_ref[pl.ds(pl.multiple_of(kb * TK, TK), TK), :]
        for r0 in range(first_row, TQ, PV_ROWS):
            rows = slice(r0, r0 + PV_ROWS)
            for c in range(2):
                pv = jnp.dot(p_ref[slot, c, rows, :], v_blk, preferred_element_type=F32)
                alpha = a_ref[slot, c, rows, :]
                scale = jnp.concatenate([alpha] * (2 * DA_HD // LANES), axis=-1)
                acc_ref[c, rows, :] = scale * acc_ref[c, rows, :] + pv

    probs(0, 0, None, diagonal_if_first=True)

    def body(t, carry):
        probs(2 * t + 1, 1, None)
        values(2 * t, 0)
        probs(2 * t + 2, 0, None)
        values(2 * t + 1, 1)
        return carry

    lax.fori_loop(0, qi - 1, body, 0)

    @pl.when(qi > 0)
    def _():
        probs(2 * qi - 1, 1, None)
        values(2 * qi - 2, 0)
        probs(2 * qi, 0, 0)
        values(2 * qi - 1, 1)

    probs(2 * qi + 1, 1, TK)
    values(2 * qi, 0)
    values(2 * qi + 1, 1, first_row=TK)

    lam = (jnp.exp(jnp.sum(lq1_ref[...] * lk1_ref[...], axis=-1, keepdims=True))
           - jnp.exp(jnp.sum(lq2_ref[...] * lk2_ref[...], axis=-1, keepdims=True))
           + lam_init)
    l0 = jnp.sum(l_ref[0], axis=-1, keepdims=True)
    l1 = jnp.sum(l_ref[1], axis=-1, keepdims=True)
    o = acc_ref[0] / l0 - lam * (acc_ref[1] / l1)
    r = lax.rsqrt(jnp.mean(o * o, axis=-1, keepdims=True) + EPS)
    y = (o * r * sg_ref[...]) * (1.0 - lam_init) * g_ref[...].astype(F32)
    o_ref[...] = y.astype(o_ref.dtype)


def _attention(q, kt, vb, gates, lq1, lk1, lq2, lk2, subln_g, lam_init, batch, seq):
    t = q.shape[0]
    nq = seq // TQ
    nk = seq // TK
    hw = 2 * DA_HD
    vec = pl.BlockSpec((1, DA_HD), lambda b, h, i: (0, 0))
    return pl.pallas_call(
        functools.partial(_attn_kernel, lam_init),
        out_shape=jax.ShapeDtypeStruct((t, D_B), BF16),
        grid=(batch, DA_HEADS, nq),
        in_specs=[pl.BlockSpec((TQ, hw), lambda b, h, i: (b * nq + i, h)),
                  pl.BlockSpec((nk, hw, TK), lambda b, h, i: (b, h, 0)),
                  pl.BlockSpec((seq, hw), lambda b, h, i: (b, h)),
                  pl.BlockSpec((TQ, hw), lambda b, h, i: (b * nq + i, DA_HEADS + h)),
                  vec, vec, vec, vec,
                  pl.BlockSpec((1, hw), lambda b, h, i: (0, 0))],
        out_specs=pl.BlockSpec((TQ, hw), lambda b, h, i: (b * nq + i, h)),
        scratch_shapes=[pltpu.VMEM((2, 2, TQ, TK), BF16),
                        pltpu.VMEM((2, 2, TQ, LANES), F32),
                        pltpu.VMEM((2, TQ, hw), F32),
                        pltpu.VMEM((2, TQ, LANES), F32),
                        pltpu.VMEM((2, TQ, LANES), F32)],
        compiler_params=_params("arbitrary", "arbitrary", "arbitrary"),
        name="diff_attention",
    )(q, kt, vb, gates, lq1.reshape(1, DA_HD), lk1.reshape(1, DA_HD),
      lq2.reshape(1, DA_HD), lk2.reshape(1, DA_HD), subln_g.reshape(1, hw))


def _outproj_kernel(ya_ref, yb_ref, wa_ref, wb_ref, x_ref, o_ref):
    acc = jnp.dot(ya_ref[...], wa_ref[...], preferred_element_type=F32)
    acc = acc + jnp.dot(yb_ref[...], wb_ref[...], preferred_element_type=F32)
    o_ref[...] = x_ref[...] + acc


def _outproj(ya, yb, w, layer, x):
    t, d = x.shape
    return pl.pallas_call(
        _outproj_kernel,
        out_shape=jax.ShapeDtypeStruct((t, d), F32),
        grid=(d // BN, t // BM),
        in_specs=[pl.BlockSpec((BM, D_A), lambda j, i: (i, 0)),
                  pl.BlockSpec((BM, D_B), lambda j, i: (i, 0)),
                  pl.BlockSpec((None, D_A, BN), lambda j, i: (layer, 0, j)),
                  pl.BlockSpec((None, D_B, BN), lambda j, i: (layer, 1, j)),
                  pl.BlockSpec((BM, BN), lambda j, i: (i, j))],
        out_specs=pl.BlockSpec((BM, BN), lambda j, i: (i, j)),
        compiler_params=_params("arbitrary", "arbitrary"),
        name="outproj",
    )(ya, yb, w, w, x)


def kernel(x, positions, norm_g, w_in, ln_g, ln_b, sgu_w, sgu_b, lam_q1, lam_k1, lam_q2, lam_k2,
           subln_g, w_out, final_g):
    batch, seq, d = x.shape
    t = batch * seq
    x = x.reshape(t, d)
    rope_c, rope_s1, rope_s2 = _rope_tables(positions)
    rope = (rope_c, rope_s1, rope_s2)
    w_in = w_in.astype(BF16)
    w_out = w_out.astype(BF16)
    nb = D_A // BN
    for l in range(DEPTH):
        h = _rmsnorm(x, norm_g[l], BF16, "rmsnorm_in")
        guv = _inproj(h, w_in, l, lambda j: j, 2 * nb,
                      functools.partial(_inproj_elementwise_kernel, _gelu), name="inproj_gelu")
        gates = _inproj(h, w_in, l, lambda j: jnp.where(j < nb, 2 * nb + j, 5 * nb + j), 2 * nb,
                        functools.partial(_inproj_elementwise_kernel, _silu), name="inproj_silu")
        q = _inproj(h, w_in, l, lambda j: 3 * nb + j, nb,
                    functools.partial(_inproj_rope_kernel, DA_HD ** -0.5 * math.log2(math.e), False),
                    extra=rope, name="inproj_q")
        kt = _inproj(h, w_in, l, lambda j: 4 * nb + j, nb,
                     functools.partial(_inproj_rope_kernel, 1.0, True),
                     extra=rope, transposed=True, name="inproj_k")
        vb = _inproj(h, w_in, l, lambda j: 5 * nb + j, nb,
                     functools.partial(_inproj_elementwise_kernel, _identity), name="inproj_v")
        ya = _sgu(guv, gates, ln_g[l], ln_b[l], sgu_w[l], sgu_b[l])
        lam_init = 0.8 - 0.6 * math.exp(-0.3 * l)
        yb = _attention(q, kt, vb, gates, lam_q1[l], lam_k1[l], lam_q2[l], lam_k2[l], subln_g[l],
                        lam_init, batch, seq)
        x = _outproj(ya, yb, w_out, l, x)
    out = _rmsnorm(x, final_g, F32, "rmsnorm_out")
    return out.reshape(batch, seq, d)
```

```python
import functools
import math

import jax
import jax.numpy as jnp
from jax import lax
from jax.experimental import pallas as pl
from jax.experimental.pallas import tpu as pltpu

D_MODEL = 4096
DEPTH = 2
CHUNK = 64
D_A = D_MODEL // 2
D_B = D_MODEL - D_A
SGU_BLOCK = 128
SGU_GROUPS = 8
SGU_GW = D_A // SGU_GROUPS
DA_HEADS = 8
DA_HD = D_B // DA_HEADS // 2
ROT_DIM = DA_HD // 4
ROPE_THETA = 500000.0
EPS = 1e-5

LANES = 128
VMEM_LIMIT = 56 * 1024 * 1024
NEG_BIG = -1e30

BM = 1024
BN = 1024
NORM_ROWS = 256
SGU_ROWS = 256
TQ = 1024
TK = TQ // 2
ATT_ROWS = 128
PV_ROWS = 512

F32 = jnp.float32
BF16 = jnp.bfloat16


def _params(*sem):
    return pltpu.CompilerParams(dimension_semantics=sem, vmem_limit_bytes=VMEM_LIMIT)


def _rmsnorm_kernel(x_ref, g_ref, o_ref):
    x = x_ref[...]
    r = lax.rsqrt(jnp.mean(x * x, axis=-1, keepdims=True) + EPS)
    o_ref[...] = (x * r * g_ref[...]).astype(o_ref.dtype)


def _rmsnorm(x, g, out_dtype, name):
    t, d = x.shape
    return pl.pallas_call(
        _rmsnorm_kernel,
        out_shape=jax.ShapeDtypeStruct((t, d), out_dtype),
        grid=(t // NORM_ROWS,),
        in_specs=[pl.BlockSpec((NORM_ROWS, d), lambda i: (i, 0)),
                  pl.BlockSpec((1, d), lambda i: (0, 0))],
        out_specs=pl.BlockSpec((NORM_ROWS, d), lambda i: (i, 0)),
        compiler_params=_params("arbitrary"),
        name=name,
    )(x, g.reshape(1, d))


def _rope_table_kernel(pos_ref, freq_ref, c_ref, s1_ref, s2_ref):
    ang = pos_ref[...] * freq_ref[...]
    cos = jnp.cos(ang)
    sin = jnp.sin(ang)
    lane = lax.broadcasted_iota(jnp.int32, ang.shape, 1)
    half = ROT_DIM // 2
    c_ref[...] = jnp.where(lane < ROT_DIM, cos, 1.0)
    s1_ref[...] = jnp.where(lane < half, -sin, 0.0)
    s2_ref[...] = jnp.where((lane >= half) & (lane < ROT_DIM), sin, 0.0)


def _rope_tables(positions):
    t = positions.size
    rows = 2048
    inv_freq = ROPE_THETA ** (-jnp.arange(0, ROT_DIM, 2, dtype=F32) / ROT_DIM)
    freq_row = jnp.zeros((1, LANES), F32).at[0, :ROT_DIM].set(jnp.tile(inv_freq, 2))
    pos = positions.astype(F32).reshape(t, 1)
    tab = jax.ShapeDtypeStruct((t, LANES), F32)
    spec = pl.BlockSpec((rows, LANES), lambda i: (i, 0))
    return pl.pallas_call(
        _rope_table_kernel,
        out_shape=(tab, tab, tab),
        grid=(t // rows,),
        in_specs=[pl.BlockSpec((rows, 1), lambda i: (i, 0)),
                  pl.BlockSpec((1, LANES), lambda i: (0, 0))],
        out_specs=(spec, spec, spec),
        compiler_params=_params("arbitrary"),
        name="rope_tables",
    )(pos, freq_row)


def _gelu(a):
    return 0.5 * a * (1.0 + lax.erf(a * (1.0 / math.sqrt(2.0))))


def _silu(a):
    return a / (1.0 + jnp.exp(-a))


def _identity(a):
    return a


def _inproj_elementwise_kernel(fn, x_ref, w_ref, o_ref):
    acc = jnp.dot(x_ref[...], w_ref[...], preferred_element_type=F32)
    o_ref[...] = fn(acc).astype(o_ref.dtype)


def _inproj_rope_kernel(scale, transposed, x_ref, w_ref, c_ref, s1_ref, s2_ref, o_ref):
    acc = jnp.dot(x_ref[...], w_ref[...], preferred_element_type=F32)
    c = c_ref[...] * scale
    s1 = s1_ref[...] * scale
    s2 = s2_ref[...] * scale
    half = ROT_DIM // 2
    for g in range(BN // LANES):
        cols = slice(g * LANES, (g + 1) * LANES)
        a = acc[:, cols]
        up = pltpu.roll(a, LANES - half, 1)
        down = pltpu.roll(a, half, 1)
        out = a * c + up * s1 + down * s2
        if transposed:
            for kb in range(BM // TK):
                o_ref[kb, cols, :] = out[kb * TK:(kb + 1) * TK, :].T.astype(o_ref.dtype)
        else:
            o_ref[:, cols] = out.astype(o_ref.dtype)


def _inproj(h, w, layer, col_block, n_blocks, body, extra=(), transposed=False, name=None):
    t, d = h.shape
    extra_specs = [pl.BlockSpec((BM, LANES), lambda j, i: (i, 0)) for _ in extra]
    if transposed:
        out_shape = jax.ShapeDtypeStruct((t // TK, n_blocks * BN, TK), BF16)
        out_spec = pl.BlockSpec((BM // TK, BN, TK), lambda j, i: (i, j, 0))
    else:
        out_shape = jax.ShapeDtypeStruct((t, n_blocks * BN), BF16)
        out_spec = pl.BlockSpec((BM, BN), lambda j, i: (i, j))
    return pl.pallas_call(
        body,
        out_shape=out_shape,
        grid=(n_blocks, t // BM),
        in_specs=[pl.BlockSpec((BM, d), lambda j, i: (i, 0)),
                  pl.BlockSpec((None, d, BN), lambda j, i: (layer, 0, col_block(j)))] + extra_specs,
        out_specs=out_spec,
        compiler_params=_params("arbitrary", "arbitrary"),
        name=name,
    )(h, w, *extra)


def _sgu_kernel(gu_ref, gv_ref, sg_ref, lng_ref, lnb_ref, w_ref, b_ref, o_ref):
    v = gv_ref[...].astype(F32)
    mu = jnp.mean(v, axis=-1, keepdims=True)
    xc = v - mu
    var = jnp.mean(xc * xc, axis=-1, keepdims=True)
    vn = (xc * lax.rsqrt(var + EPS) * lng_ref[...] + lnb_ref[...]).astype(BF16)
    row = lax.broadcasted_iota(jnp.int32, (SGU_BLOCK, SGU_BLOCK), 0)
    col = lax.broadcasted_iota(jnp.int32, (SGU_BLOCK, SGU_BLOCK), 1)
    mask = (col // CHUNK) <= (row // CHUNK)
    for g in range(SGU_GROUPS):
        wg = jnp.where(mask, w_ref[g], 0.0).astype(BF16)
        bg = b_ref[:, g:g + 1]
        cs = slice(g * SGU_GW, (g + 1) * SGU_GW)
        for r in range(SGU_ROWS // SGU_BLOCK):
            rs = slice(r * SGU_BLOCK, (r + 1) * SGU_BLOCK)
            mix = jnp.dot(wg, vn[rs, cs], preferred_element_type=F32) + bg
            y = gu_ref[rs, cs].astype(F32) * mix * sg_ref[rs, cs].astype(F32)
            o_ref[rs, cs] = y.astype(o_ref.dtype)


def _sgu(guv, sgate, ln_g, ln_b, sgu_w, sgu_b):
    t = guv.shape[0]
    return pl.pallas_call(
        _sgu_kernel,
        out_shape=jax.ShapeDtypeStruct((t, D_A), BF16),
        grid=(t // SGU_ROWS,),
        in_specs=[pl.BlockSpec((SGU_ROWS, D_A), lambda i: (i, 0)),
                  pl.BlockSpec((SGU_ROWS, D_A), lambda i: (i, 1)),
                  pl.BlockSpec((SGU_ROWS, D_A), lambda i: (i, 0)),
                  pl.BlockSpec((1, D_A), lambda i: (0, 0)),
                  pl.BlockSpec((1, D_A), lambda i: (0, 0)),
                  pl.BlockSpec((SGU_GROUPS, SGU_BLOCK, SGU_BLOCK), lambda i: (0, 0, 0)),
                  pl.BlockSpec((SGU_BLOCK, SGU_GROUPS), lambda i: (0, 0))],
        out_specs=pl.BlockSpec((SGU_ROWS, D_A), lambda i: (i, 0)),
        compiler_params=_params("arbitrary"),
        name="sgu",
    )(guv, guv, sgate, ln_g.reshape(1, D_A), ln_b.reshape(1, D_A), sgu_w, sgu_b.T)


def _attn_kernel(lam_init, q_ref, kt_ref, v_ref, g_ref, lq1_ref, lk1_ref, lq2_ref, lk2_ref, sg_ref,
                 o_ref, p_ref, d_ref, ps_ref, acc_ref, m_ref, l_ref):
    qi = pl.program_id(2)

    m_ref[...] = jnp.full(m_ref.shape, NEG_BIG, F32)
    l_ref[...] = jnp.zeros(l_ref.shape, F32)
    acc_ref[...] = jnp.zeros(acc_ref.shape, F32)

    def probs(kb, slot, col0, diagonal_if_first=False):
        for r0 in range(0, TQ, ATT_ROWS):
            if col0 is not None and r0 + ATT_ROWS <= col0:
                continue
            rows = slice(r0, r0 + ATT_ROWS)
            masked = diagonal_if_first or (col0 is not None and col0 + TK > r0)
            if masked:
                row = lax.broadcasted_iota(jnp.int32, (ATT_ROWS, TK), 0) + r0
                col = lax.broadcasted_iota(jnp.int32, (ATT_ROWS, TK), 1) + (col0 or 0)
                row_chunk = row // CHUNK
                if diagonal_if_first:
                    row_chunk = row_chunk + jnp.where(qi == 0, 0, TQ)
                mask = (col // CHUNK) <= row_chunk
            for c in range(2):
                cs = slice(c * DA_HD, (c + 1) * DA_HD)
                s = jnp.dot(q_ref[rows, cs], kt_ref[kb, cs, :], preferred_element_type=F32)
                if masked:
                    s = jnp.where(mask, s, NEG_BIG)
                m_prev = m_ref[c, rows, :]
                m_new = jnp.maximum(m_prev, jnp.max(s, axis=-1, keepdims=True))
                ps = [jnp.exp2(s[:, j * LANES:(j + 1) * LANES] - m_new) for j in range(TK // LANES)]
                m_ref[c, rows, :] = m_new
                d_ref[slot, c, rows, :] = m_prev - m_new
                ps_ref[slot, c, rows, :] = functools.reduce(jnp.add, ps)
                p_ref[slot, c, rows, :] = jnp.concatenate(ps, axis=-1).astype(BF16)

    def values(kb, slot, first_row=0):
        v_blk = v_ref[pl.ds(pl.multiple_of(kb * TK, TK), TK), :]
        for r0 in range(first_row, TQ, PV_ROWS):
            rows = slice(r0, r0 + PV_ROWS)
            for c in range(2):
                pv = jnp.dot(p_ref[slot, c, rows, :], v_blk, preferred_element_type=F32)
                alpha = jnp.exp2(d_ref[slot, c, rows, :])
                l_ref[c, rows, :] = alpha * l_ref[c, rows, :] + ps_ref[slot, c, rows, :]
                scale = jnp.concatenate([alpha] * (2 * DA_HD // LANES), axis=-1)
                acc_ref[c, rows, :] = scale * acc_ref[c, rows, :] + pv

    probs(0, 0, None, diagonal_if_first=True)

    def body(t, carry):
        probs(2 * t + 1, 1, None)
        values(2 * t, 0)
        probs(2 * t + 2, 0, None)
        values(2 * t + 1, 1)
        return carry

    lax.fori_loop(0, qi - 1, body, 0)

    @pl.when(qi > 0)
    def _():
        probs(2 * qi - 1, 1, None)
        values(2 * qi - 2, 0)
        probs(2 * qi, 0, 0)
        values(2 * qi - 1, 1)

    probs(2 * qi + 1, 1, TK)
    values(2 * qi, 0)
    values(2 * qi + 1, 1, first_row=TK)

    lam = (jnp.exp(jnp.sum(lq1_ref[...] * lk1_ref[...], axis=-1, keepdims=True))
           - jnp.exp(jnp.sum(lq2_ref[...] * lk2_ref[...], axis=-1, keepdims=True))
           + lam_init)
    l0 = jnp.sum(l_ref[0], axis=-1, keepdims=True)
    l1 = jnp.sum(l_ref[1], axis=-1, keepdims=True)
    o = acc_ref[0] / l0 - lam * (acc_ref[1] / l1)
    r = lax.rsqrt(jnp.mean(o * o, axis=-1, keepdims=True) + EPS)
    y = (o * r * sg_ref[...]) * (1.0 - lam_init) * g_ref[...].astype(F32)
    o_ref[...] = y.astype(o_ref.dtype)


def _attention(q, kt, vb, gates, lq1, lk1, lq2, lk2, subln_g, lam_init, batch, seq):
    t = q.shape[0]
    nq = seq // TQ
    nk = seq // TK
    hw = 2 * DA_HD
    vec = pl.BlockSpec((1, DA_HD), lambda b, h, i: (0, 0))
    return pl.pallas_call(
        functools.partial(_attn_kernel, lam_init),
        out_shape=jax.ShapeDtypeStruct((t, D_B), BF16),
        grid=(batch, DA_HEADS, nq),
        in_specs=[pl.BlockSpec((TQ, hw), lambda b, h, i: (b * nq + i, h)),
                  pl.BlockSpec((nk, hw, TK), lambda b, h, i: (b, h, 0)),
                  pl.BlockSpec((seq, hw), lambda b, h, i: (b, h)),
                  pl.BlockSpec((TQ, hw), lambda b, h, i: (b * nq + i, DA_HEADS + h)),
                  vec, vec, vec, vec,
                  pl.BlockSpec((1, hw), lambda b, h, i: (0, 0))],
        out_specs=pl.BlockSpec((TQ, hw), lambda b, h, i: (b * nq + i, h)),
        scratch_shapes=[pltpu.VMEM((2, 2, TQ, TK), BF16),
                        pltpu.VMEM((2, 2, TQ, LANES), F32),
                        pltpu.VMEM((2, 2, TQ, LANES), F32),
                        pltpu.VMEM((2, TQ, hw), F32),
                        pltpu.VMEM((2, TQ, LANES), F32),
                        pltpu.VMEM((2, TQ, LANES), F32)],
        compiler_params=_params("arbitrary", "arbitrary", "arbitrary"),
        name="diff_attention",
    )(q, kt, vb, gates, lq1.reshape(1, DA_HD), lk1.reshape(1, DA_HD),
      lq2.reshape(1, DA_HD), lk2.reshape(1, DA_HD), subln_g.reshape(1, hw))


def _outproj_kernel(ya_ref, yb_ref, wa_ref, wb_ref, x_ref, o_ref):
    acc = jnp.dot(ya_ref[...], wa_ref[...], preferred_element_type=F32)
    acc = acc + jnp.dot(yb_ref[...], wb_ref[...], preferred_element_type=F32)
    o_ref[...] = x_ref[...] + acc


def _outproj(ya, yb, w, layer, x):
    t, d = x.shape
    return pl.pallas_call(
        _outproj_kernel,
        out_shape=jax.ShapeDtypeStruct((t, d), F32),
        grid=(d // BN, t // BM),
        in_specs=[pl.BlockSpec((BM, D_A), lambda j, i: (i, 0)),
                  pl.BlockSpec((BM, D_B), lambda j, i: (i, 0)),
                  pl.BlockSpec((None, D_A, BN), lambda j, i: (layer, 0, j)),
                  pl.BlockSpec((None, D_B, BN), lambda j, i: (layer, 1, j)),
                  pl.BlockSpec((BM, BN), lambda j, i: (i, j))],
        out_specs=pl.BlockSpec((BM, BN), lambda j, i: (i, j)),
        compiler_params=_params("arbitrary", "arbitrary"),
        name="outproj",
    )(ya, yb, w, w, x)


def kernel(x, positions, norm_g, w_in, ln_g, ln_b, sgu_w, sgu_b, lam_q1, lam_k1, lam_q2, lam_k2,
           subln_g, w_out, final_g):
    batch, seq, d = x.shape
    t = batch * seq
    x = x.reshape(t, d)
    rope_c, rope_s1, rope_s2 = _rope_tables(positions)
    rope = (rope_c, rope_s1, rope_s2)
    w_in = w_in.astype(BF16)
    w_out = w_out.astype(BF16)
    nb = D_A // BN
    for l in range(DEPTH):
        h = _rmsnorm(x, norm_g[l], BF16, "rmsnorm_in")
        guv = _inproj(h, w_in, l, lambda j: j, 2 * nb,
                      functools.partial(_inproj_elementwise_kernel, _gelu), name="inproj_gelu")
        gates = _inproj(h, w_in, l, lambda j: jnp.where(j < nb, 2 * nb + j, 5 * nb + j), 2 * nb,
                        functools.partial(_inproj_elementwise_kernel, _silu), name="inproj_silu")
        q = _inproj(h, w_in, l, lambda j: 3 * nb + j, nb,
                    functools.partial(_inproj_rope_kernel, DA_HD ** -0.5 * math.log2(math.e), False),
                    extra=rope, name="inproj_q")
        kt = _inproj(h, w_in, l, lambda j: 4 * nb + j, nb,
                     functools.partial(_inproj_rope_kernel, 1.0, True),
                     extra=rope, transposed=True, name="inproj_k")
        vb = _inproj(h, w_in, l, lambda j: 5 * nb + j, nb,
                     functools.partial(_inproj_elementwise_kernel, _identity), name="inproj_v")
        ya = _sgu(guv, gates, ln_g[l], ln_b[l], sgu_w[l], sgu_b[l])
        lam_init = 0.8 - 0.6 * math.exp(-0.3 * l)
        yb = _attention(q, kt, vb, gates, lam_q1[l], lam_k1[l], lam_q2[l], lam_k2[l], subln_g[l],
                        lam_init, batch, seq)
        x = _outproj(ya, yb, w_out, l, x)
    out = _rmsnorm(x, final_g, F32, "rmsnorm_out")
    return out.reshape(batch, seq, d)
```

```python
import functools
import math

import jax
import jax.numpy as jnp
from jax import lax
from jax.experimental import pallas as pl
from jax.experimental.pallas import tpu as pltpu

D_MODEL = 4096
DEPTH = 2
CHUNK = 64
D_A = D_MODEL // 2
D_B = D_MODEL - D_A
SGU_BLOCK = 128
SGU_GROUPS = 8
SGU_GW = D_A // SGU_GROUPS
DA_HEADS = 8
DA_HD = D_B // DA_HEADS // 2
ROT_DIM = DA_HD // 4
ROPE_THETA = 500000.0
EPS = 1e-5

LANES = 128
VMEM_LIMIT = 56 * 1024 * 1024
NEG_BIG = -1e30

BM = 1024
BN = 1024
W_CHUNK = 512
W_PIECES = D_MODEL // W_CHUNK
NORM_ROWS = 256
SGU_ROWS = 256
TQ = 1024
TK = TQ // 2
ATT_ROWS = 128
PV_ROWS = 512

F32 = jnp.float32
BF16 = jnp.bfloat16


def _params(*sem):
    return pltpu.CompilerParams(dimension_semantics=sem, vmem_limit_bytes=VMEM_LIMIT)


def _rmsnorm_kernel(x_ref, g_ref, o_ref):
    x = x_ref[...]
    r = lax.rsqrt(jnp.mean(x * x, axis=-1, keepdims=True) + EPS)
    o_ref[...] = (x * r * g_ref[...]).astype(o_ref.dtype)


def _rmsnorm(x, g, out_dtype, name):
    t, d = x.shape
    return pl.pallas_call(
        _rmsnorm_kernel,
        out_shape=jax.ShapeDtypeStruct((t, d), out_dtype),
        grid=(t // NORM_ROWS,),
        in_specs=[pl.BlockSpec((NORM_ROWS, d), lambda i: (i, 0)),
                  pl.BlockSpec((1, d), lambda i: (0, 0))],
        out_specs=pl.BlockSpec((NORM_ROWS, d), lambda i: (i, 0)),
        compiler_params=_params("arbitrary"),
        name=name,
    )(x, g.reshape(1, d))


def _rope_table_kernel(pos_ref, freq_ref, c_ref, s1_ref, s2_ref):
    ang = pos_ref[...] * freq_ref[...]
    cos = jnp.cos(ang)
    sin = jnp.sin(ang)
    lane = lax.broadcasted_iota(jnp.int32, ang.shape, 1)
    half = ROT_DIM // 2
    c_ref[...] = jnp.where(lane < ROT_DIM, cos, 1.0)
    s1_ref[...] = jnp.where(lane < half, -sin, 0.0)
    s2_ref[...] = jnp.where((lane >= half) & (lane < ROT_DIM), sin, 0.0)


def _rope_tables(positions):
    t = positions.size
    rows = 2048
    inv_freq = ROPE_THETA ** (-jnp.arange(0, ROT_DIM, 2, dtype=F32) / ROT_DIM)
    freq_row = jnp.zeros((1, LANES), F32).at[0, :ROT_DIM].set(jnp.tile(inv_freq, 2))
    pos = positions.astype(F32).reshape(t, 1)
    tab = jax.ShapeDtypeStruct((t, LANES), F32)
    spec = pl.BlockSpec((rows, LANES), lambda i: (i, 0))
    return pl.pallas_call(
        _rope_table_kernel,
        out_shape=(tab, tab, tab),
        grid=(t // rows,),
        in_specs=[pl.BlockSpec((rows, 1), lambda i: (i, 0)),
                  pl.BlockSpec((1, LANES), lambda i: (0, 0))],
        out_specs=(spec, spec, spec),
        compiler_params=_params("arbitrary"),
        name="rope_tables",
    )(pos, freq_row)


def _gelu(a):
    return 0.5 * a * (1.0 + lax.erf(a * (1.0 / math.sqrt(2.0))))


def _silu(a):
    return a / (1.0 + jnp.exp(-a))


def _identity(a):
    return a


def _stage_weights(w_hbm, layer, col_block, n_blocks, wbf_ref, stage_ref, sem_ref):
    j = pl.program_id(0)
    i = pl.program_id(1)

    def piece(block, c, slot):
        col = pl.multiple_of(col_block(block) * BN, BN)
        row = pl.multiple_of(c * W_CHUNK, W_CHUNK)
        return pltpu.make_async_copy(w_hbm.at[layer, pl.ds(row, W_CHUNK), pl.ds(col, BN)],
                                     stage_ref.at[slot], sem_ref.at[slot])

    def convert(block, c, slot):
        row = pl.multiple_of(c * W_CHUNK, W_CHUNK)
        wbf_ref[block % 2, pl.ds(row, W_CHUNK), :] = stage_ref[slot].astype(BF16)

    @pl.when((j == 0) & (i == 0))
    def _():
        piece(0, 0, 0).start()
        for c in range(W_PIECES):
            if c + 1 < W_PIECES:
                piece(0, c + 1, (c + 1) % 2).start()
            piece(0, c, c % 2).wait()
            convert(0, c, c % 2)

    @pl.when(j + 1 < n_blocks)
    def _():
        @pl.when((i >= 1) & (i <= W_PIECES))
        def _():
            piece(j + 1, i - 1, (i - 1) % 2).wait()
            convert(j + 1, i - 1, (i - 1) % 2)

        @pl.when(i < W_PIECES)
        def _():
            piece(j + 1, i, i % 2).start()


def _inproj_kernel(epilogue, layer, col_block, n_blocks, x_ref, w_hbm, *refs):
    *extra, o_ref, wbf_ref, stage_ref, sem_ref = refs
    _stage_weights(w_hbm, layer, col_block, n_blocks, wbf_ref, stage_ref, sem_ref)
    acc = jnp.dot(x_ref[...], wbf_ref[pl.program_id(0) % 2], preferred_element_type=F32)
    epilogue(acc, *extra, o_ref)


def _elementwise_epilogue(fn, acc, o_ref):
    o_ref[...] = fn(acc).astype(o_ref.dtype)


def _rope_epilogue(scale, transposed, acc, c_ref, s1_ref, s2_ref, o_ref):
    c = c_ref[...] * scale
    s1 = s1_ref[...] * scale
    s2 = s2_ref[...] * scale
    half = ROT_DIM // 2
    for g in range(BN // LANES):
        cols = slice(g * LANES, (g + 1) * LANES)
        a = acc[:, cols]
        up = pltpu.roll(a, LANES - half, 1)
        down = pltpu.roll(a, half, 1)
        out = a * c + up * s1 + down * s2
        if transposed:
            for kb in range(BM // TK):
                o_ref[kb, cols, :] = out[kb * TK:(kb + 1) * TK, :].T.astype(o_ref.dtype)
        else:
            o_ref[:, cols] = out.astype(o_ref.dtype)


def _inproj(h, w, layer, col_block, n_blocks, epilogue, extra=(), transposed=False, name=None):
    t, d = h.shape
    assert d == W_PIECES * W_CHUNK and t // BM > W_PIECES
    extra_specs = [pl.BlockSpec((BM, LANES), lambda j, i: (i, 0)) for _ in extra]
    if transposed:
        out_shape = jax.ShapeDtypeStruct((t // TK, n_blocks * BN, TK), BF16)
        out_spec = pl.BlockSpec((BM // TK, BN, TK), lambda j, i: (i, j, 0))
    else:
        out_shape = jax.ShapeDtypeStruct((t, n_blocks * BN), BF16)
        out_spec = pl.BlockSpec((BM, BN), lambda j, i: (i, j))
    return pl.pallas_call(
        functools.partial(_inproj_kernel, epilogue, layer, col_block, n_blocks),
        out_shape=out_shape,
        grid=(n_blocks, t // BM),
        in_specs=[pl.BlockSpec((BM, d), lambda j, i: (i, 0)),
                  pl.BlockSpec(memory_space=pl.ANY)] + extra_specs,
        out_specs=out_spec,
        scratch_shapes=[pltpu.VMEM((2, d, BN), BF16),
                        pltpu.VMEM((2, W_CHUNK, BN), F32),
                        pltpu.SemaphoreType.DMA((2,))],
        compiler_params=_params("arbitrary", "arbitrary"),
        name=name,
    )(h, w, *extra)


def _sgu_kernel(gu_ref, gv_ref, sg_ref, lng_ref, lnb_ref, w_ref, b_ref, o_ref):
    v = gv_ref[...].astype(F32)
    mu = jnp.mean(v, axis=-1, keepdims=True)
    xc = v - mu
    var = jnp.mean(xc * xc, axis=-1, keepdims=True)
    vn = (xc * lax.rsqrt(var + EPS) * lng_ref[...] + lnb_ref[...]).astype(BF16)
    row = lax.broadcasted_iota(jnp.int32, (SGU_BLOCK, SGU_BLOCK), 0)
    col = lax.broadcasted_iota(jnp.int32, (SGU_BLOCK, SGU_BLOCK), 1)
    mask = (col // CHUNK) <= (row // CHUNK)
    for g in range(SGU_GROUPS):
        wg = jnp.where(mask, w_ref[g], 0.0).astype(BF16)
        bg = b_ref[:, g:g + 1]
        cs = slice(g * SGU_GW, (g + 1) * SGU_GW)
        for r in range(SGU_ROWS // SGU_BLOCK):
            rs = slice(r * SGU_BLOCK, (r + 1) * SGU_BLOCK)
            mix = jnp.dot(wg, vn[rs, cs], preferred_element_type=F32) + bg
            y = gu_ref[rs, cs].astype(F32) * mix * sg_ref[rs, cs].astype(F32)
            o_ref[rs, cs] = y.astype(o_ref.dtype)


def _sgu(guv, sgate, ln_g, ln_b, sgu_w, sgu_b):
    t = guv.shape[0]
    return pl.pallas_call(
        _sgu_kernel,
        out_shape=jax.ShapeDtypeStruct((t, D_A), BF16),
        grid=(t // SGU_ROWS,),
        in_specs=[pl.BlockSpec((SGU_ROWS, D_A), lambda i: (i, 0)),
                  pl.BlockSpec((SGU_ROWS, D_A), lambda i: (i, 1)),
                  pl.BlockSpec((SGU_ROWS, D_A), lambda i: (i, 0)),
                  pl.BlockSpec((1, D_A), lambda i: (0, 0)),
                  pl.BlockSpec((1, D_A), lambda i: (0, 0)),
                  pl.BlockSpec((SGU_GROUPS, SGU_BLOCK, SGU_BLOCK), lambda i: (0, 0, 0)),
                  pl.BlockSpec((SGU_BLOCK, SGU_GROUPS), lambda i: (0, 0))],
        out_specs=pl.BlockSpec((SGU_ROWS, D_A), lambda i: (i, 0)),
        compiler_params=_params("arbitrary"),
        name="sgu",
    )(guv, guv, sgate, ln_g.reshape(1, D_A), ln_b.reshape(1, D_A), sgu_w, sgu_b.T)


def _attn_kernel(lam_init, q_ref, kt_ref, v_ref, g_ref, lq1_ref, lk1_ref, lq2_ref, lk2_ref, sg_ref,
                 o_ref, p_ref, d_ref, ps_ref, acc_ref, m_ref, l_ref):
    qi = pl.program_id(2)

    m_ref[...] = jnp.full(m_ref.shape, NEG_BIG, F32)
    l_ref[...] = jnp.zeros(l_ref.shape, F32)
    acc_ref[...] = jnp.zeros(acc_ref.shape, F32)

    def probs(kb, slot, col0, diagonal_if_first=False):
        for r0 in range(0, TQ, ATT_ROWS):
            if col0 is not None and r0 + ATT_ROWS <= col0:
                continue
            rows = slice(r0, r0 + ATT_ROWS)
            masked = diagonal_if_first or (col0 is not None and col0 + TK > r0)
            if masked:
                row = lax.broadcasted_iota(jnp.int32, (ATT_ROWS, TK), 0) + r0
                col = lax.broadcasted_iota(jnp.int32, (ATT_ROWS, TK), 1) + (col0 or 0)
                row_chunk = row // CHUNK
                if diagonal_if_first:
                    row_chunk = row_chunk + jnp.where(qi == 0, 0, TQ)
                mask = (col // CHUNK) <= row_chunk
            for c in range(2):
                cs = slice(c * DA_HD, (c + 1) * DA_HD)
                s = jnp.dot(q_ref[rows, cs], kt_ref[kb, cs, :], preferred_element_type=F32)
                if masked:
                    s = jnp.where(mask, s, NEG_BIG)
                m_prev = m_ref[c, rows, :]
                m_new = jnp.maximum(m_prev, jnp.max(s, axis=-1, keepdims=True))
                ps = [jnp.exp2(s[:, j * LANES:(j + 1) * LANES] - m_new) for j in range(TK // LANES)]
                m_ref[c, rows, :] = m_new
                d_ref[slot, c, rows, :] = m_prev - m_new
                ps_ref[slot, c, rows, :] = functools.reduce(jnp.add, ps)
                p_ref[slot, c, rows, :] = jnp.concatenate(ps, axis=-1).astype(BF16)

    def values(kb, slot, first_row=0):
        v_blk = v_ref[pl.ds(pl.multiple_of(kb * TK, TK), TK), :]
        for r0 in range(first_row, TQ, PV_ROWS):
            rows = slice(r0, r0 + PV_ROWS)
            for c in range(2):
                pv = jnp.dot(p_ref[slot, c, rows, :], v_blk, preferred_element_type=F32)
                alpha = jnp.exp2(d_ref[slot, c, rows, :])
                l_ref[c, rows, :] = alpha * l_ref[c, rows, :] + ps_ref[slot, c, rows, :]
                scale = jnp.concatenate([alpha] * (2 * DA_HD // LANES), axis=-1)
                acc_ref[c, rows, :] = scale * acc_ref[c, rows, :] + pv

    probs(0, 0, None, diagonal_if_first=True)

    def body(t, carry):
        probs(2 * t + 1, 1, None)
        values(2 * t, 0)
        probs(2 * t + 2, 0, None)
        values(2 * t + 1, 1)
        return carry

    lax.fori_loop(0, qi - 1, body, 0)

    @pl.when(qi > 0)
    def _():
        probs(2 * qi - 1, 1, None)
        values(2 * qi - 2, 0)
        probs(2 * qi, 0, 0)
        values(2 * qi - 1, 1)

    probs(2 * qi + 1, 1, TK)
    values(2 * qi, 0)
    values(2 * qi + 1, 1, first_row=TK)

    lam = (jnp.exp(jnp.sum(lq1_ref[...] * lk1_ref[...], axis=-1, keepdims=True))
           - jnp.exp(jnp.sum(lq2_ref[...] * lk2_ref[...], axis=-1, keepdims=True))
           + lam_init)
    l0 = jnp.sum(l_ref[0], axis=-1, keepdims=True)
    l1 = jnp.sum(l_ref[1], axis=-1, keepdims=True)
    o = acc_ref[0] / l0 - lam * (acc_ref[1] / l1)
    r = lax.rsqrt(jnp.mean(o * o, axis=-1, keepdims=True) + EPS)
    y = (o * r * sg_ref[...]) * (1.0 - lam_init) * g_ref[...].astype(F32)
    o_ref[...] = y.astype(o_ref.dtype)


def _attention(q, kt, vb, gates, lq1, lk1, lq2, lk2, subln_g, lam_init, batch, seq):
    t = q.shape[0]
    nq = seq // TQ
    nk = seq // TK
    hw = 2 * DA_HD
    vec = pl.BlockSpec((1, DA_HD), lambda b, h, i: (0, 0))
    return pl.pallas_call(
        functools.partial(_attn_kernel, lam_init),
        out_shape=jax.ShapeDtypeStruct((t, D_B), BF16),
        grid=(batch, DA_HEADS, nq),
        in_specs=[pl.BlockSpec((TQ, hw), lambda b, h, i: (b * nq + i, h)),
                  pl.BlockSpec((nk, hw, TK), lambda b, h, i: (b, h, 0)),
                  pl.BlockSpec((seq, hw), lambda b, h, i: (b, h)),
                  pl.BlockSpec((TQ, hw), lambda b, h, i: (b * nq + i, DA_HEADS + h)),
                  vec, vec, vec, vec,
                  pl.BlockSpec((1, hw), lambda b, h, i: (0, 0))],
        out_specs=pl.BlockSpec((TQ, hw), lambda b, h, i: (b * nq + i, h)),
        scratch_shapes=[pltpu.VMEM((2, 2, TQ, TK), BF16),
                        pltpu.VMEM((2, 2, TQ, LANES), F32),
                        pltpu.VMEM((2, 2, TQ, LANES), F32),
                        pltpu.VMEM((2, TQ, hw), F32),
                        pltpu.VMEM((2, TQ, LANES), F32),
                        pltpu.VMEM((2, TQ, LANES), F32)],
        compiler_params=_params("arbitrary", "arbitrary", "arbitrary"),
        name="diff_attention",
    )(q, kt, vb, gates, lq1.reshape(1, DA_HD), lk1.reshape(1, DA_HD),
      lq2.reshape(1, DA_HD), lk2.reshape(1, DA_HD), subln_g.reshape(1, hw))


def _outproj_kernel(ya_ref, yb_ref, wa_ref, wb_ref, x_ref, o_ref):
    acc = jnp.dot(ya_ref[...], wa_ref[...], preferred_element_type=F32)
    acc = acc + jnp.dot(yb_ref[...], wb_ref[...], preferred_element_type=F32)
    o_ref[...] = x_ref[...] + acc


def _outproj(ya, yb, w, layer, x):
    t, d = x.shape
    return pl.pallas_call(
        _outproj_kernel,
        out_shape=jax.ShapeDtypeStruct((t, d), F32),
        grid=(d // BN, t // BM),
        in_specs=[pl.BlockSpec((BM, D_A), lambda j, i: (i, 0)),
                  pl.BlockSpec((BM, D_B), lambda j, i: (i, 0)),
                  pl.BlockSpec((None, D_A, BN), lambda j, i: (layer, 0, j)),
                  pl.BlockSpec((None, D_B, BN), lambda j, i: (layer, 1, j)),
                  pl.BlockSpec((BM, BN), lambda j, i: (i, j))],
        out_specs=pl.BlockSpec((BM, BN), lambda j, i: (i, j)),
        compiler_params=_params("arbitrary", "arbitrary"),
        name="outproj",
    )(ya, yb, w, w, x)


def kernel(x, positions, norm_g, w_in, ln_g, ln_b, sgu_w, sgu_b, lam_q1, lam_k1, lam_q2, lam_k2,
           subln_g, w_out, final_g):
    batch, seq, d = x.shape
    t = batch * seq
    x = x.reshape(t, d)
    rope_c, rope_s1, rope_s2 = _rope_tables(positions)
    rope = (rope_c, rope_s1, rope_s2)
    w_out = w_out.astype(BF16)
    nb = D_A // BN
    for l in range(DEPTH):
        h = _rmsnorm(x, norm_g[l], BF16, "rmsnorm_in")
        guv = _inproj(h, w_in, l, lambda j: j, 2 * nb,
                      functools.partial(_elementwise_epilogue, _gelu), name="inproj_gelu")
        gates = _inproj(h, w_in, l, lambda j: jnp.where(j < nb, 2 * nb + j, 5 * nb + j), 2 * nb,
                        functools.partial(_elementwise_epilogue, _silu), name="inproj_silu")
        q = _inproj(h, w_in, l, lambda j: 3 * nb + j, nb,
                    functools.partial(_rope_epilogue, DA_HD ** -0.5 * math.log2(math.e), False),
                    extra=rope, name="inproj_q")
        kt = _inproj(h, w_in, l, lambda j: 4 * nb + j, nb,
                     functools.partial(_rope_epilogue, 1.0, True),
                     extra=rope, transposed=True, name="inproj_k")
        vb = _inproj(h, w_in, l, lambda j: 5 * nb + j, nb,
                     functools.partial(_elementwise_epilogue, _identity), name="inproj_v")
        ya = _sgu(guv, gates, ln_g[l], ln_b[l], sgu_w[l], sgu_b[l])
        lam_init = 0.8 - 0.6 * math.exp(-0.3 * l)
        yb = _attention(q, kt, vb, gates, lam_q1[l], lam_k1[l], lam_q2[l], lam_k2[l], subln_g[l],
                        lam_init, batch, seq)
        x = _outproj(ya, yb, w_out, l, x)
    out = _rmsnorm(x, final_g, F32, "rmsnorm_out")
    return out.reshape(batch, seq, d)
```

```python
import functools
import math

import jax
import jax.numpy as jnp
from jax import lax
from jax.experimental import pallas as pl
from jax.experimental.pallas import tpu as pltpu

D_MODEL = 4096
DEPTH = 2
CHUNK = 64
D_A = D_MODEL // 2
D_B = D_MODEL - D_A
SGU_BLOCK = 128
SGU_GROUPS = 8
SGU_GW = D_A // SGU_GROUPS
DA_HEADS = 8
DA_HD = D_B // DA_HEADS // 2
ROT_DIM = DA_HD // 4
ROPE_THETA = 500000.0
EPS = 1e-5

LANES = 128
VMEM_LIMIT = 56 * 1024 * 1024
NEG_BIG = -1e30

BM = 1024
BN = 1024
W_CHUNK = 512
W_PIECES = D_MODEL // W_CHUNK
NORM_ROWS = 512
SGU_ROWS = 512
TQ = 1024
TK = TQ // 2
ATT_ROWS = 128
PV_ROWS = 512

F32 = jnp.float32
BF16 = jnp.bfloat16


def _params(*sem):
    return pltpu.CompilerParams(dimension_semantics=sem, vmem_limit_bytes=VMEM_LIMIT)


def _rmsnorm_kernel(x_ref, g_ref, o_ref):
    x = x_ref[...]
    r = lax.rsqrt(jnp.mean(x * x, axis=-1, keepdims=True) + EPS)
    o_ref[...] = (x * r * g_ref[...]).astype(o_ref.dtype)


def _rmsnorm(x, g, out_dtype, name):
    t, d = x.shape
    return pl.pallas_call(
        _rmsnorm_kernel,
        out_shape=jax.ShapeDtypeStruct((t, d), out_dtype),
        grid=(t // NORM_ROWS,),
        in_specs=[pl.BlockSpec((NORM_ROWS, d), lambda i: (i, 0)),
                  pl.BlockSpec((1, d), lambda i: (0, 0))],
        out_specs=pl.BlockSpec((NORM_ROWS, d), lambda i: (i, 0)),
        compiler_params=_params("arbitrary"),
        name=name,
    )(x, g.reshape(1, d))


def _rope_table_kernel(pos_ref, freq_ref, c_ref, s1_ref, s2_ref):
    ang = pos_ref[...] * freq_ref[...]
    cos = jnp.cos(ang)
    sin = jnp.sin(ang)
    lane = lax.broadcasted_iota(jnp.int32, ang.shape, 1)
    half = ROT_DIM // 2
    c_ref[...] = jnp.where(lane < ROT_DIM, cos, 1.0)
    s1_ref[...] = jnp.where(lane < half, -sin, 0.0)
    s2_ref[...] = jnp.where((lane >= half) & (lane < ROT_DIM), sin, 0.0)


def _rope_tables(positions):
    t = positions.size
    rows = 2048
    inv_freq = ROPE_THETA ** (-jnp.arange(0, ROT_DIM, 2, dtype=F32) / ROT_DIM)
    freq_row = jnp.zeros((1, LANES), F32).at[0, :ROT_DIM].set(jnp.tile(inv_freq, 2))
    pos = positions.astype(F32).reshape(t, 1)
    tab = jax.ShapeDtypeStruct((t, LANES), F32)
    spec = pl.BlockSpec((rows, LANES), lambda i: (i, 0))
    return pl.pallas_call(
        _rope_table_kernel,
        out_shape=(tab, tab, tab),
        grid=(t // rows,),
        in_specs=[pl.BlockSpec((rows, 1), lambda i: (i, 0)),
                  pl.BlockSpec((1, LANES), lambda i: (0, 0))],
        out_specs=(spec, spec, spec),
        compiler_params=_params("arbitrary"),
        name="rope_tables",
    )(pos, freq_row)


def _gelu(a):
    return 0.5 * a * (1.0 + lax.erf(a * (1.0 / math.sqrt(2.0))))


def _silu(a):
    return a / (1.0 + jnp.exp(-a))


def _identity(a):
    return a


def _stage_weights(w_hbm, layer, col_block, n_blocks, wbf_ref, stage_ref, sem_ref):
    j = pl.program_id(0)
    i = pl.program_id(1)

    def piece(block, c, slot):
        col = pl.multiple_of(col_block(block) * BN, BN)
        row = pl.multiple_of(c * W_CHUNK, W_CHUNK)
        return pltpu.make_async_copy(w_hbm.at[layer, pl.ds(row, W_CHUNK), pl.ds(col, BN)],
                                     stage_ref.at[slot], sem_ref.at[slot])

    def convert(block, c, slot):
        row = pl.multiple_of(c * W_CHUNK, W_CHUNK)
        wbf_ref[block % 2, pl.ds(row, W_CHUNK), :] = stage_ref[slot].astype(BF16)

    @pl.when((j == 0) & (i == 0))
    def _():
        piece(0, 0, 0).start()
        for c in range(W_PIECES):
            if c + 1 < W_PIECES:
                piece(0, c + 1, (c + 1) % 2).start()
            piece(0, c, c % 2).wait()
            convert(0, c, c % 2)

    @pl.when(j + 1 < n_blocks)
    def _():
        @pl.when((i >= 1) & (i <= W_PIECES))
        def _():
            piece(j + 1, i - 1, (i - 1) % 2).wait()
            convert(j + 1, i - 1, (i - 1) % 2)

        @pl.when(i < W_PIECES)
        def _():
            piece(j + 1, i, i % 2).start()


def _inproj_kernel(epilogue, layer, col_block, n_blocks, row_scaled, x_ref, w_hbm, *refs):
    *extra, o_ref, wbf_ref, stage_ref, sem_ref = refs
    _stage_weights(w_hbm, layer, col_block, n_blocks, wbf_ref, stage_ref, sem_ref)
    acc = jnp.dot(x_ref[...], wbf_ref[pl.program_id(0) % 2], preferred_element_type=F32)
    if row_scaled:
        r_ref, *extra = extra
        acc = acc * jnp.concatenate([r_ref[...]] * (BN // LANES), axis=-1)
    epilogue(acc, *extra, o_ref)


def _elementwise_epilogue(fn, acc, o_ref):
    o_ref[...] = fn(acc).astype(o_ref.dtype)


def _rope_epilogue(scale, transposed, acc, c_ref, s1_ref, s2_ref, o_ref):
    c = c_ref[...] * scale
    s1 = s1_ref[...] * scale
    s2 = s2_ref[...] * scale
    half = ROT_DIM // 2
    for g in range(BN // LANES):
        cols = slice(g * LANES, (g + 1) * LANES)
        a = acc[:, cols]
        up = pltpu.roll(a, LANES - half, 1)
        down = pltpu.roll(a, half, 1)
        out = a * c + up * s1 + down * s2
        if transposed:
            for kb in range(BM // TK):
                o_ref[kb, cols, :] = out[kb * TK:(kb + 1) * TK, :].T.astype(o_ref.dtype)
        else:
            o_ref[:, cols] = out.astype(o_ref.dtype)


def _inproj(h, row_scale, w, layer, col_block, n_blocks, epilogue, extra=(), transposed=False, name=None):
    t, d = h.shape
    assert d == W_PIECES * W_CHUNK and t // BM > W_PIECES
    if row_scale is not None:
        extra = (row_scale,) + tuple(extra)
    extra_specs = [pl.BlockSpec((BM, LANES), lambda j, i: (i, 0)) for _ in extra]
    if transposed:
        out_shape = jax.ShapeDtypeStruct((t // TK, n_blocks * BN, TK), BF16)
        out_spec = pl.BlockSpec((BM // TK, BN, TK), lambda j, i: (i, j, 0))
    else:
        out_shape = jax.ShapeDtypeStruct((t, n_blocks * BN), BF16)
        out_spec = pl.BlockSpec((BM, BN), lambda j, i: (i, j))
    return pl.pallas_call(
        functools.partial(_inproj_kernel, epilogue, layer, col_block, n_blocks, row_scale is not None),
        out_shape=out_shape,
        grid=(n_blocks, t // BM),
        in_specs=[pl.BlockSpec((BM, d), lambda j, i: (i, 0)),
                  pl.BlockSpec(memory_space=pl.ANY)] + extra_specs,
        out_specs=out_spec,
        scratch_shapes=[pltpu.VMEM((2, d, BN), BF16),
                        pltpu.VMEM((2, W_CHUNK, BN), F32),
                        pltpu.SemaphoreType.DMA((2,))],
        compiler_params=_params("arbitrary", "arbitrary"),
        name=name,
    )(h, w, *extra)


def _sgu_kernel(gu_ref, gv_ref, sg_ref, lng_ref, lnb_ref, w_ref, b_ref, o_ref):
    v = gv_ref[...].astype(F32)
    mu = jnp.mean(v, axis=-1, keepdims=True)
    xc = v - mu
    var = jnp.mean(xc * xc, axis=-1, keepdims=True)
    vn = (xc * lax.rsqrt(var + EPS) * lng_ref[...] + lnb_ref[...]).astype(BF16)
    row = lax.broadcasted_iota(jnp.int32, (SGU_BLOCK, SGU_BLOCK), 0)
    col = lax.broadcasted_iota(jnp.int32, (SGU_BLOCK, SGU_BLOCK), 1)
    mask = (col // CHUNK) <= (row // CHUNK)
    for g in range(SGU_GROUPS):
        wg = jnp.where(mask, w_ref[g], 0.0).astype(BF16)
        bg = b_ref[:, g:g + 1]
        cs = slice(g * SGU_GW, (g + 1) * SGU_GW)
        for r in range(SGU_ROWS // SGU_BLOCK):
            rs = slice(r * SGU_BLOCK, (r + 1) * SGU_BLOCK)
            mix = jnp.dot(wg, vn[rs, cs], preferred_element_type=F32) + bg
            y = gu_ref[rs, cs].astype(F32) * mix * sg_ref[rs, cs].astype(F32)
            o_ref[rs, cs] = y.astype(o_ref.dtype)


def _sgu(guv, sgate, ln_g, ln_b, sgu_w, sgu_b):
    t = guv.shape[0]
    return pl.pallas_call(
        _sgu_kernel,
        out_shape=jax.ShapeDtypeStruct((t, D_A), BF16),
        grid=(t // SGU_ROWS,),
        in_specs=[pl.BlockSpec((SGU_ROWS, D_A), lambda i: (i, 0)),
                  pl.BlockSpec((SGU_ROWS, D_A), lambda i: (i, 1)),
                  pl.BlockSpec((SGU_ROWS, D_A), lambda i: (i, 0)),
                  pl.BlockSpec((1, D_A), lambda i: (0, 0)),
                  pl.BlockSpec((1, D_A), lambda i: (0, 0)),
                  pl.BlockSpec((SGU_GROUPS, SGU_BLOCK, SGU_BLOCK), lambda i: (0, 0, 0)),
                  pl.BlockSpec((SGU_BLOCK, SGU_GROUPS), lambda i: (0, 0))],
        out_specs=pl.BlockSpec((SGU_ROWS, D_A), lambda i: (i, 0)),
        compiler_params=_params("arbitrary"),
        name="sgu",
    )(guv, guv, sgate, ln_g.reshape(1, D_A), ln_b.reshape(1, D_A), sgu_w, sgu_b.T)


def _attn_kernel(lam_init, q_ref, kt_ref, v_ref, g_ref, lq1_ref, lk1_ref, lq2_ref, lk2_ref, sg_ref,
                 o_ref, p_ref, d_ref, ps_ref, acc_ref, m_ref, l_ref):
    qi = pl.program_id(2)

    m_ref[...] = jnp.full(m_ref.shape, NEG_BIG, F32)
    l_ref[...] = jnp.zeros(l_ref.shape, F32)
    acc_ref[...] = jnp.zeros(acc_ref.shape, F32)

    def probs(kb, slot, col0, diagonal_if_first=False):
        for r0 in range(0, TQ, ATT_ROWS):
            if col0 is not None and r0 + ATT_ROWS <= col0:
                continue
            rows = slice(r0, r0 + ATT_ROWS)
            masked = diagonal_if_first or (col0 is not None and col0 + TK > r0)
            if masked:
                row = lax.broadcasted_iota(jnp.int32, (ATT_ROWS, TK), 0) + r0
                col = lax.broadcasted_iota(jnp.int32, (ATT_ROWS, TK), 1) + (col0 or 0)
                row_chunk = row // CHUNK
                if diagonal_if_first:
                    row_chunk = row_chunk + jnp.where(qi == 0, 0, TQ)
                mask = (col // CHUNK) <= row_chunk
            for c in range(2):
                cs = slice(c * DA_HD, (c + 1) * DA_HD)
                s = jnp.dot(q_ref[rows, cs], kt_ref[kb, cs, :], preferred_element_type=F32)
                if masked:
                    s = jnp.where(mask, s, NEG_BIG)
                m_prev = m_ref[c, rows, :]
                m_new = jnp.maximum(m_prev, jnp.max(s, axis=-1, keepdims=True))
                ps = [jnp.exp2(s[:, j * LANES:(j + 1) * LANES] - m_new) for j in range(TK // LANES)]
                m_ref[c, rows, :] = m_new
                d_ref[slot, c, rows, :] = m_prev - m_new
                ps_ref[slot, c, rows, :] = functools.reduce(jnp.add, ps)
                p_ref[slot, c, rows, :] = jnp.concatenate(ps, axis=-1).astype(BF16)

    def values(kb, slot, first_row=0):
        v_blk = v_ref[pl.ds(pl.multiple_of(kb * TK, TK), TK), :]
        for r0 in range(first_row, TQ, PV_ROWS):
            rows = slice(r0, r0 + PV_ROWS)
            for c in range(2):
                pv = jnp.dot(p_ref[slot, c, rows, :], v_blk, preferred_element_type=F32)
                alpha = jnp.exp2(d_ref[slot, c, rows, :])
                l_ref[c, rows, :] = alpha * l_ref[c, rows, :] + ps_ref[slot, c, rows, :]
                scale = jnp.concatenate([alpha] * (2 * DA_HD // LANES), axis=-1)
                acc_ref[c, rows, :] = scale * acc_ref[c, rows, :] + pv

    probs(0, 0, None, diagonal_if_first=True)

    def body(t, carry):
        probs(2 * t + 1, 1, None)
        values(2 * t, 0)
        probs(2 * t + 2, 0, None)
        values(2 * t + 1, 1)
        return carry

    lax.fori_loop(0, qi - 1, body, 0)

    @pl.when(qi > 0)
    def _():
        probs(2 * qi - 1, 1, None)
        values(2 * qi - 2, 0)
        probs(2 * qi, 0, 0)
        values(2 * qi - 1, 1)

    lam = (jnp.exp(jnp.sum(lq1_ref[...] * lk1_ref[...], axis=-1, keepdims=True))
           - jnp.exp(jnp.sum(lq2_ref[...] * lk2_ref[...], axis=-1, keepdims=True))
           + lam_init)

    def finish(rows):
        l0 = jnp.sum(l_ref[0, rows, :], axis=-1, keepdims=True)
        l1 = jnp.sum(l_ref[1, rows, :], axis=-1, keepdims=True)
        o = acc_ref[0, rows, :] / l0 - lam * (acc_ref[1, rows, :] / l1)
        r = lax.rsqrt(jnp.mean(o * o, axis=-1, keepdims=True) + EPS)
        y = (o * r * sg_ref[...]) * (1.0 - lam_init) * g_ref[rows, :].astype(F32)
        o_ref[rows, :] = y.astype(o_ref.dtype)

    probs(2 * qi + 1, 1, TK)
    values(2 * qi, 0)
    finish(slice(0, TK))
    values(2 * qi + 1, 1, first_row=TK)
    finish(slice(TK, TQ))


def _attention(q, kt, vb, gates, lq1, lk1, lq2, lk2, subln_g, lam_init, batch, seq):
    t = q.shape[0]
    nq = seq // TQ
    nk = seq // TK
    hw = 2 * DA_HD
    vec = pl.BlockSpec((1, DA_HD), lambda b, h, i: (0, 0))
    return pl.pallas_call(
        functools.partial(_attn_kernel, lam_init),
        out_shape=jax.ShapeDtypeStruct((t, D_B), BF16),
        grid=(batch, DA_HEADS, nq),
        in_specs=[pl.BlockSpec((TQ, hw), lambda b, h, i: (b * nq + i, h)),
                  pl.BlockSpec((nk, hw, TK), lambda b, h, i: (b, h, 0)),
                  pl.BlockSpec((seq, hw), lambda b, h, i: (b, h)),
                  pl.BlockSpec((TQ, hw), lambda b, h, i: (b * nq + i, DA_HEADS + h)),
                  vec, vec, vec, vec,
                  pl.BlockSpec((1, hw), lambda b, h, i: (0, 0))],
        out_specs=pl.BlockSpec((TQ, hw), lambda b, h, i: (b * nq + i, h)),
        scratch_shapes=[pltpu.VMEM((2, 2, TQ, TK), BF16),
                        pltpu.VMEM((2, 2, TQ, LANES), F32),
                        pltpu.VMEM((2, 2, TQ, LANES), F32),
                        pltpu.VMEM((2, TQ, hw), F32),
                        pltpu.VMEM((2, TQ, LANES), F32),
                        pltpu.VMEM((2, TQ, LANES), F32)],
        compiler_params=_params("arbitrary", "arbitrary", "arbitrary"),
        name="diff_attention",
    )(q, kt, vb, gates, lq1.reshape(1, DA_HD), lk1.reshape(1, DA_HD),
      lq2.reshape(1, DA_HD), lk2.reshape(1, DA_HD), subln_g.reshape(1, hw))


def _outproj_kernel(ya_ref, yb_ref, wa_ref, wb_ref, x_ref, *refs):
    acc = jnp.dot(ya_ref[...], wa_ref[...], preferred_element_type=F32)
    acc = acc + jnp.dot(yb_ref[...], wb_ref[...], preferred_element_type=F32)
    x_new = x_ref[...] + acc
    if len(refs) == 1:
        (o_ref,) = refs
    else:
        g_ref, o_ref, xg_ref, ssq_ref = refs
        xg_ref[...] = (x_new * g_ref[...]).astype(xg_ref.dtype)
        sq = x_new * x_new
        ssq_ref[...] = functools.reduce(jnp.add, [sq[:, k * LANES:(k + 1) * LANES] for k in range(BN // LANES)])
    o_ref[...] = x_new


def _outproj(ya, yb, w, layer, x, next_gain=None):
    t, d = x.shape
    wmode = {} if next_gain is None else dict(pipeline_mode=pl.Buffered(1))
    in_specs = [pl.BlockSpec((BM, D_A), lambda j, i: (i, 0)),
                pl.BlockSpec((BM, D_B), lambda j, i: (i, 0)),
                pl.BlockSpec((None, D_A, BN), lambda j, i: (layer, 0, j), **wmode),
                pl.BlockSpec((None, D_B, BN), lambda j, i: (layer, 1, j), **wmode),
                pl.BlockSpec((BM, BN), lambda j, i: (i, j))]
    out_shape = jax.ShapeDtypeStruct((t, d), F32)
    out_specs = pl.BlockSpec((BM, BN), lambda j, i: (i, j))
    args = (ya, yb, w, w, x)
    if next_gain is not None:
        in_specs.append(pl.BlockSpec((1, BN), lambda j, i: (0, j)))
        args += (next_gain.reshape(1, d),)
        out_shape = (out_shape, jax.ShapeDtypeStruct((t, d), BF16),
                     jax.ShapeDtypeStruct((d // BN, t, LANES), F32))
        out_specs = (out_specs, pl.BlockSpec((BM, BN), lambda j, i: (i, j)),
                     pl.BlockSpec((None, BM, LANES), lambda j, i: (j, i, 0)))
    return pl.pallas_call(
        _outproj_kernel,
        out_shape=out_shape,
        grid=(d // BN, t // BM),
        in_specs=in_specs,
        out_specs=out_specs,
        compiler_params=_params("arbitrary", "arbitrary"),
        name="outproj",
    )(*args)


def _row_scale_kernel(ssq_ref, o_ref):
    total = jnp.sum(functools.reduce(jnp.add, [ssq_ref[k] for k in range(ssq_ref.shape[0])]),
                    axis=-1, keepdims=True)
    o_ref[...] = jnp.broadcast_to(lax.rsqrt(total * (1.0 / D_MODEL) + EPS), o_ref.shape)


def _row_scale(ssq):
    nparts, t, _ = ssq.shape
    rows = 2048
    return pl.pallas_call(
        _row_scale_kernel,
        out_shape=jax.ShapeDtypeStruct((t, LANES), F32),
        grid=(t // rows,),
        in_specs=[pl.BlockSpec((nparts, rows, LANES), lambda i: (0, i, 0))],
        out_specs=pl.BlockSpec((rows, LANES), lambda i: (i, 0)),
        compiler_params=_params("arbitrary"),
        name="row_scale",
    )(ssq)


def kernel(x, positions, norm_g, w_in, ln_g, ln_b, sgu_w, sgu_b, lam_q1, lam_k1, lam_q2, lam_k2,
           subln_g, w_out, final_g):
    batch, seq, d = x.shape
    t = batch * seq
    x = x.reshape(t, d)
    rope_c, rope_s1, rope_s2 = _rope_tables(positions)
    rope = (rope_c, rope_s1, rope_s2)
    w_out = w_out.astype(BF16)
    nb = D_A // BN
    h = _rmsnorm(x, norm_g[0], BF16, "rmsnorm_in")
    row_scale = None
    for l in range(DEPTH):
        guv = _inproj(h, row_scale, w_in, l, lambda j: j, 2 * nb,
                      functools.partial(_elementwise_epilogue, _gelu), name="inproj_gelu")
        gates = _inproj(h, row_scale, w_in, l, lambda j: jnp.where(j < nb, 2 * nb + j, 5 * nb + j), 2 * nb,
                        functools.partial(_elementwise_epilogue, _silu), name="inproj_silu")
        q = _inproj(h, row_scale, w_in, l, lambda j: 3 * nb + j, nb,
                    functools.partial(_rope_epilogue, DA_HD ** -0.5 * math.log2(math.e), False),
                    extra=rope, name="inproj_q")
        kt = _inproj(h, row_scale, w_in, l, lambda j: 4 * nb + j, nb,
                     functools.partial(_rope_epilogue, 1.0, True),
                     extra=rope, transposed=True, name="inproj_k")
        vb = _inproj(h, row_scale, w_in, l, lambda j: 5 * nb + j, nb,
                     functools.partial(_elementwise_epilogue, _identity), name="inproj_v")
        ya = _sgu(guv, gates, ln_g[l], ln_b[l], sgu_w[l], sgu_b[l])
        lam_init = 0.8 - 0.6 * math.exp(-0.3 * l)
        yb = _attention(q, kt, vb, gates, lam_q1[l], lam_k1[l], lam_q2[l], lam_k2[l], subln_g[l],
                        lam_init, batch, seq)
        if l + 1 < DEPTH:
            x, h, ssq = _outproj(ya, yb, w_out, l, x, next_gain=norm_g[l + 1])
            row_scale = _row_scale(ssq)
        else:
            x = _outproj(ya, yb, w_out, l, x)
    out = _rmsnorm(x, final_g, F32, "rmsnorm_out")
    return out.reshape(batch, seq, d)
```

```python
import functools
import math

import jax
import jax.numpy as jnp
from jax import lax
from jax.experimental import pallas as pl
from jax.experimental.pallas import tpu as pltpu

D_MODEL = 4096
DEPTH = 2
CHUNK = 64
D_A = D_MODEL // 2
D_B = D_MODEL - D_A
SGU_BLOCK = 128
SGU_GROUPS = 8
SGU_GW = D_A // SGU_GROUPS
DA_HEADS = 8
DA_HD = D_B // DA_HEADS // 2
ROT_DIM = DA_HD // 4
ROPE_THETA = 500000.0
EPS = 1e-5

LANES = 128
VMEM_LIMIT = 56 * 1024 * 1024
NEG_BIG = -1e30

BM = 1024
BN = 1024
OUT_PIECE = 256
W_CHUNK = 512
W_PIECES = D_MODEL // W_CHUNK
NORM_ROWS = 512
SGU_ROWS = 512
TQ = 1024
TK = TQ // 2
ATT_ROWS = 128
PV_ROWS = 512

F32 = jnp.float32
BF16 = jnp.bfloat16


def _params(*sem):
    return pltpu.CompilerParams(dimension_semantics=sem, vmem_limit_bytes=VMEM_LIMIT)


def _rmsnorm_kernel(x_ref, g_ref, o_ref):
    x = x_ref[...]
    r = lax.rsqrt(jnp.mean(x * x, axis=-1, keepdims=True) + EPS)
    o_ref[...] = (x * r * g_ref[...]).astype(o_ref.dtype)


def _rmsnorm(x, g, out_dtype, name):
    t, d = x.shape
    return pl.pallas_call(
        _rmsnorm_kernel,
        out_shape=jax.ShapeDtypeStruct((t, d), out_dtype),
        grid=(t // NORM_ROWS,),
        in_specs=[pl.BlockSpec((NORM_ROWS, d), lambda i: (i, 0)),
                  pl.BlockSpec((1, d), lambda i: (0, 0))],
        out_specs=pl.BlockSpec((NORM_ROWS, d), lambda i: (i, 0)),
        compiler_params=_params("arbitrary"),
        name=name,
    )(x, g.reshape(1, d))


def _rope_table_kernel(pos_ref, freq_ref, c_ref, s1_ref, s2_ref):
    ang = pos_ref[...] * freq_ref[...]
    cos = jnp.cos(ang)
    sin = jnp.sin(ang)
    lane = lax.broadcasted_iota(jnp.int32, ang.shape, 1)
    half = ROT_DIM // 2
    c_ref[...] = jnp.where(lane < ROT_DIM, cos, 1.0)
    s1_ref[...] = jnp.where(lane < half, -sin, 0.0)
    s2_ref[...] = jnp.where((lane >= half) & (lane < ROT_DIM), sin, 0.0)


def _rope_tables(positions):
    t = positions.size
    rows = 2048
    inv_freq = ROPE_THETA ** (-jnp.arange(0, ROT_DIM, 2, dtype=F32) / ROT_DIM)
    freq_row = jnp.zeros((1, LANES), F32).at[0, :ROT_DIM].set(jnp.tile(inv_freq, 2))
    pos = positions.astype(F32).reshape(t, 1)
    tab = jax.ShapeDtypeStruct((t, LANES), F32)
    spec = pl.BlockSpec((rows, LANES), lambda i: (i, 0))
    return pl.pallas_call(
        _rope_table_kernel,
        out_shape=(tab, tab, tab),
        grid=(t // rows,),
        in_specs=[pl.BlockSpec((rows, 1), lambda i: (i, 0)),
                  pl.BlockSpec((1, LANES), lambda i: (0, 0))],
        out_specs=(spec, spec, spec),
        compiler_params=_params("arbitrary"),
        name="rope_tables",
    )(pos, freq_row)


def _gelu(a):
    return 0.5 * a * (1.0 + lax.erf(a * (1.0 / math.sqrt(2.0))))


def _silu(a):
    return a / (1.0 + jnp.exp(-a))


def _identity(a):
    return a


def _stage_weights(w_hbm, layer, col_block, n_blocks, wbf_ref, stage_ref, sem_ref):
    j = pl.program_id(0)
    i = pl.program_id(1)

    def piece(block, c, slot):
        col = pl.multiple_of(col_block(block) * BN, BN)
        row = pl.multiple_of(c * W_CHUNK, W_CHUNK)
        return pltpu.make_async_copy(w_hbm.at[layer, pl.ds(row, W_CHUNK), pl.ds(col, BN)],
                                     stage_ref.at[slot], sem_ref.at[slot])

    def convert(block, c, slot):
        row = pl.multiple_of(c * W_CHUNK, W_CHUNK)
        wbf_ref[block % 2, pl.ds(row, W_CHUNK), :] = stage_ref[slot].astype(BF16)

    @pl.when((j == 0) & (i == 0))
    def _():
        piece(0, 0, 0).start()
        for c in range(W_PIECES):
            if c + 1 < W_PIECES:
                piece(0, c + 1, (c + 1) % 2).start()
            piece(0, c, c % 2).wait()
            convert(0, c, c % 2)

    @pl.when(j + 1 < n_blocks)
    def _():
        @pl.when((i >= 1) & (i <= W_PIECES))
        def _():
            piece(j + 1, i - 1, (i - 1) % 2).wait()
            convert(j + 1, i - 1, (i - 1) % 2)

        @pl.when(i < W_PIECES)
        def _():
            piece(j + 1, i, i % 2).start()


def _inproj_kernel(epilogue, layer, col_block, n_blocks, row_scaled, piece, x_ref, w_hbm, *refs):
    *extra, o_ref, wbf_ref, stage_ref, sem_ref = refs
    _stage_weights(w_hbm, layer, col_block, n_blocks, wbf_ref, stage_ref, sem_ref)
    w = wbf_ref[pl.program_id(0) % 2]
    if row_scaled:
        r_ref, *extra = extra
    for r0 in range(0, BM, piece):
        rows = slice(r0, r0 + piece)
        acc = jnp.dot(x_ref[rows, :], w, preferred_element_type=F32)
        if row_scaled:
            acc = acc * jnp.concatenate([r_ref[rows, :]] * (BN // LANES), axis=-1)
        epilogue(acc, rows, *extra, o_ref)


def _elementwise_epilogue(fn, acc, rows, o_ref):
    o_ref[rows, :] = fn(acc).astype(o_ref.dtype)


def _rope_epilogue(scale, transposed, acc, rows, c_ref, s1_ref, s2_ref, o_ref):
    c = c_ref[rows, :] * scale
    s1 = s1_ref[rows, :] * scale
    s2 = s2_ref[rows, :] * scale
    half = ROT_DIM // 2
    for g in range(BN // LANES):
        cols = slice(g * LANES, (g + 1) * LANES)
        a = acc[:, cols]
        up = pltpu.roll(a, LANES - half, 1)
        down = pltpu.roll(a, half, 1)
        out = a * c + up * s1 + down * s2
        if transposed:
            lanes = slice(rows.start % TK, rows.start % TK + rows.stop - rows.start)
            o_ref[rows.start // TK, cols, lanes] = out.T.astype(o_ref.dtype)
        else:
            o_ref[rows, cols] = out.astype(o_ref.dtype)


def _inproj(h, row_scale, w, layer, col_block, n_blocks, epilogue, extra=(), transposed=False,
            piece=128, name=None):
    t, d = h.shape
    assert d == W_PIECES * W_CHUNK and t // BM > W_PIECES
    if row_scale is not None:
        extra = (row_scale,) + tuple(extra)
    extra_specs = [pl.BlockSpec((BM, LANES), lambda j, i: (i, 0)) for _ in extra]
    if transposed:
        out_shape = jax.ShapeDtypeStruct((t // TK, n_blocks * BN, TK), BF16)
        out_spec = pl.BlockSpec((BM // TK, BN, TK), lambda j, i: (i, j, 0))
    else:
        out_shape = jax.ShapeDtypeStruct((t, n_blocks * BN), BF16)
        out_spec = pl.BlockSpec((BM, BN), lambda j, i: (i, j))
    return pl.pallas_call(
        functools.partial(_inproj_kernel, epilogue, layer, col_block, n_blocks, row_scale is not None, piece),
        out_shape=out_shape,
        grid=(n_blocks, t // BM),
        in_specs=[pl.BlockSpec((BM, d), lambda j, i: (i, 0)),
                  pl.BlockSpec(memory_space=pl.ANY)] + extra_specs,
        out_specs=out_spec,
        scratch_shapes=[pltpu.VMEM((2, d, BN), BF16),
                        pltpu.VMEM((2, W_CHUNK, BN), F32),
                        pltpu.SemaphoreType.DMA((2,))],
        compiler_params=_params("arbitrary", "arbitrary"),
        name=name,
    )(h, w, *extra)


def _sgu_kernel(gu_ref, gv_ref, sg_ref, lng_ref, lnb_ref, w_ref, b_ref, o_ref):
    v = gv_ref[...].astype(F32)
    mu = jnp.mean(v, axis=-1, keepdims=True)
    xc = v - mu
    var = jnp.mean(xc * xc, axis=-1, keepdims=True)
    vn = (xc * lax.rsqrt(var + EPS) * lng_ref[...] + lnb_ref[...]).astype(BF16)
    row = lax.broadcasted_iota(jnp.int32, (SGU_BLOCK, SGU_BLOCK), 0)
    col = lax.broadcasted_iota(jnp.int32, (SGU_BLOCK, SGU_BLOCK), 1)
    mask = (col // CHUNK) <= (row // CHUNK)
    for g in range(SGU_GROUPS):
        wg = jnp.where(mask, w_ref[g], 0.0).astype(BF16)
        bg = b_ref[:, g:g + 1]
        cs = slice(g * SGU_GW, (g + 1) * SGU_GW)
        for r in range(SGU_ROWS // SGU_BLOCK):
            rs = slice(r * SGU_BLOCK, (r + 1) * SGU_BLOCK)
            mix = jnp.dot(wg, vn[rs, cs], preferred_element_type=F32) + bg
            y = gu_ref[rs, cs].astype(F32) * mix * sg_ref[rs, cs].astype(F32)
            o_ref[rs, cs] = y.astype(o_ref.dtype)


def _sgu(guv, sgate, ln_g, ln_b, sgu_w, sgu_b):
    t = guv.shape[0]
    return pl.pallas_call(
        _sgu_kernel,
        out_shape=jax.ShapeDtypeStruct((t, D_A), BF16),
        grid=(t // SGU_ROWS,),
        in_specs=[pl.BlockSpec((SGU_ROWS, D_A), lambda i: (i, 0)),
                  pl.BlockSpec((SGU_ROWS, D_A), lambda i: (i, 1)),
                  pl.BlockSpec((SGU_ROWS, D_A), lambda i: (i, 0)),
                  pl.BlockSpec((1, D_A), lambda i: (0, 0)),
                  pl.BlockSpec((1, D_A), lambda i: (0, 0)),
                  pl.BlockSpec((SGU_GROUPS, SGU_BLOCK, SGU_BLOCK), lambda i: (0, 0, 0)),
                  pl.BlockSpec((SGU_BLOCK, SGU_GROUPS), lambda i: (0, 0))],
        out_specs=pl.BlockSpec((SGU_ROWS, D_A), lambda i: (i, 0)),
        compiler_params=_params("arbitrary"),
        name="sgu",
    )(guv, guv, sgate, ln_g.reshape(1, D_A), ln_b.reshape(1, D_A), sgu_w, sgu_b.T)


def _attn_kernel(lam_init, q_ref, kt_ref, v_ref, g_ref, lq1_ref, lk1_ref, lq2_ref, lk2_ref, sg_ref,
                 o_ref, p_ref, d_ref, ps_ref, acc_ref, m_ref, l_ref):
    qi = pl.program_id(2)

    m_ref[...] = jnp.full(m_ref.shape, NEG_BIG, F32)
    l_ref[...] = jnp.zeros(l_ref.shape, F32)
    acc_ref[...] = jnp.zeros(acc_ref.shape, F32)

    def probs(kb, slot, col0, diagonal_if_first=False):
        for r0 in range(0, TQ, ATT_ROWS):
            if col0 is not None and r0 + ATT_ROWS <= col0:
                continue
            rows = slice(r0, r0 + ATT_ROWS)
            masked = diagonal_if_first or (col0 is not None and col0 + TK > r0)
            if masked:
                row = lax.broadcasted_iota(jnp.int32, (ATT_ROWS, TK), 0) + r0
                col = lax.broadcasted_iota(jnp.int32, (ATT_ROWS, TK), 1) + (col0 or 0)
                row_chunk = row // CHUNK
                if diagonal_if_first:
                    row_chunk = row_chunk + jnp.where(qi == 0, 0, TQ)
                mask = (col // CHUNK) <= row_chunk
            for c in range(2):
                cs = slice(c * DA_HD, (c + 1) * DA_HD)
                s = jnp.dot(q_ref[rows, cs], kt_ref[kb, cs, :], preferred_element_type=F32)
                if masked:
                    s = jnp.where(mask, s, NEG_BIG)
                m_prev = m_ref[c, rows, :]
                m_new = jnp.maximum(m_prev, jnp.max(s, axis=-1, keepdims=True))
                ps = [jnp.exp2(s[:, j * LANES:(j + 1) * LANES] - m_new) for j in range(TK // LANES)]
                m_ref[c, rows, :] = m_new
                d_ref[slot, c, rows, :] = m_prev - m_new
                ps_ref[slot, c, rows, :] = functools.reduce(jnp.add, ps)
                p_ref[slot, c, rows, :] = jnp.concatenate(ps, axis=-1).astype(BF16)

    def values(kb, slot, first_row=0):
        v_blk = v_ref[pl.ds(pl.multiple_of(kb * TK, TK), TK), :]
        for r0 in range(first_row, TQ, PV_ROWS):
            rows = slice(r0, r0 + PV_ROWS)
            for c in range(2):
                pv = jnp.dot(p_ref[slot, c, rows, :], v_blk, preferred_element_type=F32)
                alpha = jnp.exp2(d_ref[slot, c, rows, :])
                l_ref[c, rows, :] = alpha * l_ref[c, rows, :] + ps_ref[slot, c, rows, :]
                scale = jnp.concatenate([alpha] * (2 * DA_HD // LANES), axis=-1)
                acc_ref[c, rows, :] = scale * acc_ref[c, rows, :] + pv

    probs(0, 0, None, diagonal_if_first=True)

    def body(t, carry):
        probs(2 * t + 1, 1, None)
        values(2 * t, 0)
        probs(2 * t + 2, 0, None)
        values(2 * t + 1, 1)
        return carry

    lax.fori_loop(0, qi - 1, body, 0)

    @pl.when(qi > 0)
    def _():
        probs(2 * qi - 1, 1, None)
        values(2 * qi - 2, 0)
        probs(2 * qi, 0, 0)
        values(2 * qi - 1, 1)

    lam = (jnp.exp(jnp.sum(lq1_ref[...] * lk1_ref[...], axis=-1, keepdims=True))
           - jnp.exp(jnp.sum(lq2_ref[...] * lk2_ref[...], axis=-1, keepdims=True))
           + lam_init)

    def finish(rows):
        l0 = jnp.sum(l_ref[0, rows, :], axis=-1, keepdims=True)
        l1 = jnp.sum(l_ref[1, rows, :], axis=-1, keepdims=True)
        o = acc_ref[0, rows, :] / l0 - lam * (acc_ref[1, rows, :] / l1)
        r = lax.rsqrt(jnp.mean(o * o, axis=-1, keepdims=True) + EPS)
        y = (o * r * sg_ref[...]) * (1.0 - lam_init) * g_ref[rows, :].astype(F32)
        o_ref[rows, :] = y.astype(o_ref.dtype)

    probs(2 * qi + 1, 1, TK)
    values(2 * qi, 0)
    finish(slice(0, TK))
    values(2 * qi + 1, 1, first_row=TK)
    finish(slice(TK, TQ))


def _attention(q, kt, vb, gates, lq1, lk1, lq2, lk2, subln_g, lam_init, batch, seq):
    t = q.shape[0]
    nq = seq // TQ
    nk = seq // TK
    hw = 2 * DA_HD
    vec = pl.BlockSpec((1, DA_HD), lambda b, h, i: (0, 0))
    return pl.pallas_call(
        functools.partial(_attn_kernel, lam_init),
        out_shape=jax.ShapeDtypeStruct((t, D_B), BF16),
        grid=(batch, DA_HEADS, nq),
        in_specs=[pl.BlockSpec((TQ, hw), lambda b, h, i: (b * nq + i, h)),
                  pl.BlockSpec((nk, hw, TK), lambda b, h, i: (b, h, 0)),
                  pl.BlockSpec((seq, hw), lambda b, h, i: (b, h)),
                  pl.BlockSpec((TQ, hw), lambda b, h, i: (b * nq + i, DA_HEADS + h)),
                  vec, vec, vec, vec,
                  pl.BlockSpec((1, hw), lambda b, h, i: (0, 0))],
        out_specs=pl.BlockSpec((TQ, hw), lambda b, h, i: (b * nq + i, h)),
        scratch_shapes=[pltpu.VMEM((2, 2, TQ, TK), BF16),
                        pltpu.VMEM((2, 2, TQ, LANES), F32),
                        pltpu.VMEM((2, 2, TQ, LANES), F32),
                        pltpu.VMEM((2, TQ, hw), F32),
                        pltpu.VMEM((2, TQ, LANES), F32),
                        pltpu.VMEM((2, TQ, LANES), F32)],
        compiler_params=_params("arbitrary", "arbitrary", "arbitrary"),
        name="diff_attention",
    )(q, kt, vb, gates, lq1.reshape(1, DA_HD), lk1.reshape(1, DA_HD),
      lq2.reshape(1, DA_HD), lk2.reshape(1, DA_HD), subln_g.reshape(1, hw))


def _outproj_kernel(ya_ref, yb_ref, wa_ref, wb_ref, x_ref, *refs):
    for r0 in range(0, BM, OUT_PIECE):
        rows = slice(r0, r0 + OUT_PIECE)
        acc = jnp.dot(ya_ref[rows, :], wa_ref[...], preferred_element_type=F32)
        acc = acc + jnp.dot(yb_ref[rows, :], wb_ref[...], preferred_element_type=F32)
        x_new = x_ref[rows, :] + acc
        if len(refs) == 1:
            (o_ref,) = refs
        else:
            g_ref, o_ref, xg_ref, ssq_ref = refs
            xg_ref[rows, :] = (x_new * g_ref[...]).astype(xg_ref.dtype)
            sq = x_new * x_new
            ssq_ref[rows, :] = functools.reduce(
                jnp.add, [sq[:, k * LANES:(k + 1) * LANES] for k in range(BN // LANES)])
        o_ref[rows, :] = x_new


def _outproj(ya, yb, w, layer, x, next_gain=None):
    t, d = x.shape
    wmode = {} if next_gain is None else dict(pipeline_mode=pl.Buffered(1))
    in_specs = [pl.BlockSpec((BM, D_A), lambda j, i: (i, 0)),
                pl.BlockSpec((BM, D_B), lambda j, i: (i, 0)),
                pl.BlockSpec((None, D_A, BN), lambda j, i: (layer, 0, j), **wmode),
                pl.BlockSpec((None, D_B, BN), lambda j, i: (layer, 1, j), **wmode),
                pl.BlockSpec((BM, BN), lambda j, i: (i, j))]
    out_shape = jax.ShapeDtypeStruct((t, d), F32)
    out_specs = pl.BlockSpec((BM, BN), lambda j, i: (i, j))
    args = (ya, yb, w, w, x)
    if next_gain is not None:
        in_specs.append(pl.BlockSpec((1, BN), lambda j, i: (0, j)))
        args += (next_gain.reshape(1, d),)
        out_shape = (out_shape, jax.ShapeDtypeStruct((t, d), BF16),
                     jax.ShapeDtypeStruct((d // BN, t, LANES), F32))
        out_specs = (out_specs, pl.BlockSpec((BM, BN), lambda j, i: (i, j)),
                     pl.BlockSpec((None, BM, LANES), lambda j, i: (j, i, 0)))
    return pl.pallas_call(
        _outproj_kernel,
        out_shape=out_shape,
        grid=(d // BN, t // BM),
        in_specs=in_specs,
        out_specs=out_specs,
        compiler_params=_params("arbitrary", "arbitrary"),
        name="outproj",
    )(*args)


def _row_scale_kernel(ssq_ref, o_ref):
    total = jnp.sum(functools.reduce(jnp.add, [ssq_ref[k] for k in range(ssq_ref.shape[0])]),
                    axis=-1, keepdims=True)
    o_ref[...] = jnp.broadcast_to(lax.rsqrt(total * (1.0 / D_MODEL) + EPS), o_ref.shape)


def _row_scale(ssq):
    nparts, t, _ = ssq.shape
    rows = 2048
    return pl.pallas_call(
        _row_scale_kernel,
        out_shape=jax.ShapeDtypeStruct((t, LANES), F32),
        grid=(t // rows,),
        in_specs=[pl.BlockSpec((nparts, rows, LANES), lambda i: (0, i, 0))],
        out_specs=pl.BlockSpec((rows, LANES), lambda i: (i, 0)),
        compiler_params=_params("arbitrary"),
        name="row_scale",
    )(ssq)


def kernel(x, positions, norm_g, w_in, ln_g, ln_b, sgu_w, sgu_b, lam_q1, lam_k1, lam_q2, lam_k2,
           subln_g, w_out, final_g):
    batch, seq, d = x.shape
    t = batch * seq
    x = x.reshape(t, d)
    rope_c, rope_s1, rope_s2 = _rope_tables(positions)
    rope = (rope_c, rope_s1, rope_s2)
    w_out = w_out.astype(BF16)
    nb = D_A // BN
    h = _rmsnorm(x, norm_g[0], BF16, "rmsnorm_in")
    row_scale = None
    for l in range(DEPTH):
        guv = _inproj(h, row_scale, w_in, l, lambda j: j, 2 * nb,
                      functools.partial(_elementwise_epilogue, _gelu), name="inproj_gelu")
        gates = _inproj(h, row_scale, w_in, l, lambda j: jnp.where(j < nb, 2 * nb + j, 5 * nb + j), 2 * nb,
                        functools.partial(_elementwise_epilogue, _silu), name="inproj_silu")
        q = _inproj(h, row_scale, w_in, l, lambda j: 3 * nb + j, nb,
                    functools.partial(_rope_epilogue, DA_HD ** -0.5 * math.log2(math.e), False),
                    extra=rope, name="inproj_q")
        kt = _inproj(h, row_scale, w_in, l, lambda j: 4 * nb + j, nb,
                     functools.partial(_rope_epilogue, 1.0, True),
                     extra=rope, transposed=True, piece=256, name="inproj_k")
        vb = _inproj(h, row_scale, w_in, l, lambda j: 5 * nb + j, nb,
                     functools.partial(_elementwise_epilogue, _identity), name="inproj_v")
        ya = _sgu(guv, gates, ln_g[l], ln_b[l], sgu_w[l], sgu_b[l])
        lam_init = 0.8 - 0.6 * math.exp(-0.3 * l)
        yb = _attention(q, kt, vb, gates, lam_q1[l], lam_k1[l], lam_q2[l], lam_k2[l], subln_g[l],
                        lam_init, batch, seq)
        if l + 1 < DEPTH:
            x, h, ssq = _outproj(ya, yb, w_out, l, x, next_gain=norm_g[l + 1])
            row_scale = _row_scale(ssq)
        else:
            x = _outproj(ya, yb, w_out, l, x)
    out = _rmsnorm(x, final_g, F32, "rmsnorm_out")
    return out.reshape(batch, seq, d)
```

```python
import functools
import math

import jax
import jax.numpy as jnp
from jax import lax
from jax.experimental import pallas as pl
from jax.experimental.pallas import tpu as pltpu

D_MODEL = 4096
DEPTH = 2
CHUNK = 64
D_A = D_MODEL // 2
D_B = D_MODEL - D_A
SGU_BLOCK = 128
SGU_GROUPS = 8
SGU_GW = D_A // SGU_GROUPS
DA_HEADS = 8
DA_HD = D_B // DA_HEADS // 2
ROT_DIM = DA_HD // 4
ROPE_THETA = 500000.0
EPS = 1e-5

LANES = 128
VMEM_LIMIT = 56 * 1024 * 1024
NEG_BIG = -1e30

BM = 1024
BN = 1024
OUT_PIECE = 256
W_CHUNK = 512
W_PIECES = D_MODEL // W_CHUNK
NORM_ROWS = 512
SGU_ROWS = 512
TQ = 1024
TK = TQ // 2
ATT_ROWS = 128
PV_ROWS = 512

F32 = jnp.float32
BF16 = jnp.bfloat16


def _params(*sem):
    return pltpu.CompilerParams(dimension_semantics=sem, vmem_limit_bytes=VMEM_LIMIT)


def _rmsnorm_kernel(x_ref, g_ref, o_ref):
    x = x_ref[...]
    r = lax.rsqrt(jnp.mean(x * x, axis=-1, keepdims=True) + EPS)
    o_ref[...] = (x * r * g_ref[...]).astype(o_ref.dtype)


def _rmsnorm(x, g, out_dtype, name):
    t, d = x.shape
    return pl.pallas_call(
        _rmsnorm_kernel,
        out_shape=jax.ShapeDtypeStruct((t, d), out_dtype),
        grid=(t // NORM_ROWS,),
        in_specs=[pl.BlockSpec((NORM_ROWS, d), lambda i: (i, 0)),
                  pl.BlockSpec((1, d), lambda i: (0, 0))],
        out_specs=pl.BlockSpec((NORM_ROWS, d), lambda i: (i, 0)),
        compiler_params=_params("arbitrary"),
        name=name,
    )(x, g.reshape(1, d))


def _rope_table_kernel(pos_ref, freq_ref, c_ref, s1_ref, s2_ref):
    ang = pos_ref[...] * freq_ref[...]
    cos = jnp.cos(ang)
    sin = jnp.sin(ang)
    lane = lax.broadcasted_iota(jnp.int32, ang.shape, 1)
    half = ROT_DIM // 2
    c_ref[...] = jnp.where(lane < ROT_DIM, cos, 1.0)
    s1_ref[...] = jnp.where(lane < half, -sin, 0.0)
    s2_ref[...] = jnp.where((lane >= half) & (lane < ROT_DIM), sin, 0.0)


def _rope_tables(positions):
    t = positions.size
    rows = 2048
    inv_freq = ROPE_THETA ** (-jnp.arange(0, ROT_DIM, 2, dtype=F32) / ROT_DIM)
    freq_row = jnp.zeros((1, LANES), F32).at[0, :ROT_DIM].set(jnp.tile(inv_freq, 2))
    pos = positions.astype(F32).reshape(t, 1)
    tab = jax.ShapeDtypeStruct((t, LANES), F32)
    spec = pl.BlockSpec((rows, LANES), lambda i: (i, 0))
    return pl.pallas_call(
        _rope_table_kernel,
        out_shape=(tab, tab, tab),
        grid=(t // rows,),
        in_specs=[pl.BlockSpec((rows, 1), lambda i: (i, 0)),
                  pl.BlockSpec((1, LANES), lambda i: (0, 0))],
        out_specs=(spec, spec, spec),
        compiler_params=_params("arbitrary"),
        name="rope_tables",
    )(pos, freq_row)


def _gelu(a):
    return 0.5 * a * (1.0 + lax.erf(a * (1.0 / math.sqrt(2.0))))


def _silu(a):
    return a / (1.0 + jnp.exp(-a))


def _identity(a):
    return a


def _stage_weights(w_hbm, layer, col_block, n_blocks, wbf_ref, stage_ref, sem_ref):
    j = pl.program_id(0)
    i = pl.program_id(1)

    def piece(block, c, slot):
        col = pl.multiple_of(col_block(block) * BN, BN)
        row = pl.multiple_of(c * W_CHUNK, W_CHUNK)
        return pltpu.make_async_copy(w_hbm.at[layer, pl.ds(row, W_CHUNK), pl.ds(col, BN)],
                                     stage_ref.at[slot], sem_ref.at[slot])

    def convert(block, c, slot):
        row = pl.multiple_of(c * W_CHUNK, W_CHUNK)
        wbf_ref[block % 2, pl.ds(row, W_CHUNK), :] = stage_ref[slot].astype(BF16)

    @pl.when((j == 0) & (i == 0))
    def _():
        piece(0, 0, 0).start()
        for c in range(W_PIECES):
            if c + 1 < W_PIECES:
                piece(0, c + 1, (c + 1) % 2).start()
            piece(0, c, c % 2).wait()
            convert(0, c, c % 2)

    @pl.when(j + 1 < n_blocks)
    def _():
        @pl.when((i >= 1) & (i <= W_PIECES))
        def _():
            piece(j + 1, i - 1, (i - 1) % 2).wait()
            convert(j + 1, i - 1, (i - 1) % 2)

        @pl.when(i < W_PIECES)
        def _():
            piece(j + 1, i, i % 2).start()


def _inproj_kernel(epilogue, layer, col_block, n_blocks, row_scaled, piece, x_ref, w_hbm, *refs):
    *extra, o_ref, wbf_ref, stage_ref, sem_ref = refs
    _stage_weights(w_hbm, layer, col_block, n_blocks, wbf_ref, stage_ref, sem_ref)
    w = wbf_ref[pl.program_id(0) % 2]
    if row_scaled:
        r_ref, *extra = extra
    for r0 in range(0, BM, piece):
        rows = slice(r0, r0 + piece)
        acc = jnp.dot(x_ref[rows, :], w, preferred_element_type=F32)
        if row_scaled:
            acc = acc * jnp.concatenate([r_ref[rows, :]] * (BN // LANES), axis=-1)
        epilogue(acc, rows, *extra, o_ref)


def _elementwise_epilogue(fn, acc, rows, o_ref):
    o_ref[rows, :] = fn(acc).astype(o_ref.dtype)


def _rope_epilogue(scale, transposed, acc, rows, c_ref, s1_ref, s2_ref, o_ref):
    c = c_ref[rows, :] * scale
    s1 = s1_ref[rows, :] * scale
    s2 = s2_ref[rows, :] * scale
    half = ROT_DIM // 2
    for g in range(BN // LANES):
        cols = slice(g * LANES, (g + 1) * LANES)
        a = acc[:, cols]
        up = pltpu.roll(a, LANES - half, 1)
        down = pltpu.roll(a, half, 1)
        out = a * c + up * s1 + down * s2
        if transposed:
            lanes = slice(rows.start % TK, rows.start % TK + rows.stop - rows.start)
            o_ref[rows.start // TK, cols, lanes] = out.T.astype(o_ref.dtype)
        else:
            o_ref[rows, cols] = out.astype(o_ref.dtype)


def _inproj(h, row_scale, w, layer, col_block, n_blocks, epilogue, extra=(), transposed=False,
            piece=128, name=None):
    t, d = h.shape
    assert d == W_PIECES * W_CHUNK and t // BM > W_PIECES
    if row_scale is not None:
        extra = (row_scale,) + tuple(extra)
    extra_specs = [pl.BlockSpec((BM, LANES), lambda j, i: (i, 0)) for _ in extra]
    if transposed:
        out_shape = jax.ShapeDtypeStruct((t // TK, n_blocks * BN, TK), BF16)
        out_spec = pl.BlockSpec((BM // TK, BN, TK), lambda j, i: (i, j, 0))
    else:
        out_shape = jax.ShapeDtypeStruct((t, n_blocks * BN), BF16)
        out_spec = pl.BlockSpec((BM, BN), lambda j, i: (i, j))
    return pl.pallas_call(
        functools.partial(_inproj_kernel, epilogue, layer, col_block, n_blocks, row_scale is not None, piece),
        out_shape=out_shape,
        grid=(n_blocks, t // BM),
        in_specs=[pl.BlockSpec((BM, d), lambda j, i: (i, 0)),
                  pl.BlockSpec(memory_space=pl.ANY)] + extra_specs,
        out_specs=out_spec,
        scratch_shapes=[pltpu.VMEM((2, d, BN), BF16),
                        pltpu.VMEM((2, W_CHUNK, BN), F32),
                        pltpu.SemaphoreType.DMA((2,))],
        compiler_params=_params("arbitrary", "arbitrary"),
        name=name,
    )(h, w, *extra)


def _sgu_kernel(gu_ref, gv_ref, sg_ref, lng_ref, lnb_ref, w_ref, b_ref, o_ref):
    v = gv_ref[...].astype(F32)
    mu = jnp.mean(v, axis=-1, keepdims=True)
    xc = v - mu
    var = jnp.mean(xc * xc, axis=-1, keepdims=True)
    vn = (xc * lax.rsqrt(var + EPS) * lng_ref[...] + lnb_ref[...]).astype(BF16)
    row = lax.broadcasted_iota(jnp.int32, (SGU_BLOCK, SGU_BLOCK), 0)
    col = lax.broadcasted_iota(jnp.int32, (SGU_BLOCK, SGU_BLOCK), 1)
    mask = (col // CHUNK) <= (row // CHUNK)
    for g in range(SGU_GROUPS):
        wg = jnp.where(mask, w_ref[g], 0.0).astype(BF16)
        bg = b_ref[:, g:g + 1]
        cs = slice(g * SGU_GW, (g + 1) * SGU_GW)
        for r in range(SGU_ROWS // SGU_BLOCK):
            rs = slice(r * SGU_BLOCK, (r + 1) * SGU_BLOCK)
            mix = jnp.dot(wg, vn[rs, cs], preferred_element_type=F32) + bg
            y = gu_ref[rs, cs].astype(F32) * mix * sg_ref[rs, cs].astype(F32)
            o_ref[rs, cs] = y.astype(o_ref.dtype)


def _sgu(guv, sgate, ln_g, ln_b, sgu_w, sgu_b):
    t = guv.shape[0]
    return pl.pallas_call(
        _sgu_kernel,
        out_shape=jax.ShapeDtypeStruct((t, D_A), BF16),
        grid=(t // SGU_ROWS,),
        in_specs=[pl.BlockSpec((SGU_ROWS, D_A), lambda i: (i, 0)),
                  pl.BlockSpec((SGU_ROWS, D_A), lambda i: (i, 1)),
                  pl.BlockSpec((SGU_ROWS, D_A), lambda i: (i, 0)),
                  pl.BlockSpec((1, D_A), lambda i: (0, 0)),
                  pl.BlockSpec((1, D_A), lambda i: (0, 0)),
                  pl.BlockSpec((SGU_GROUPS, SGU_BLOCK, SGU_BLOCK), lambda i: (0, 0, 0)),
                  pl.BlockSpec((SGU_BLOCK, SGU_GROUPS), lambda i: (0, 0))],
        out_specs=pl.BlockSpec((SGU_ROWS, D_A), lambda i: (i, 0)),
        compiler_params=_params("arbitrary"),
        name="sgu",
    )(guv, guv, sgate, ln_g.reshape(1, D_A), ln_b.reshape(1, D_A), sgu_w, sgu_b.T)


def _attn_kernel(lam_init, q_ref, kt_ref, v_ref, g_ref, lq1_ref, lk1_ref, lq2_ref, lk2_ref, sg_ref,
                 o_ref, p_ref, d_ref, ps_ref, acc_ref, m_ref, l_ref):
    qi = pl.program_id(2)

    l_ref[...] = jnp.zeros(l_ref.shape, F32)
    acc_ref[...] = jnp.zeros(acc_ref.shape, F32)

    def probs(kb, slot, col0, first=False):
        for r0 in range(0, TQ, ATT_ROWS):
            if col0 is not None and r0 + ATT_ROWS <= col0:
                continue
            rows = slice(r0, r0 + ATT_ROWS)
            masked = col0 is not None and col0 + TK > r0
            if masked:
                row = lax.broadcasted_iota(jnp.int32, (ATT_ROWS, TK), 0) + r0
                col = lax.broadcasted_iota(jnp.int32, (ATT_ROWS, TK), 1) + col0
                mask = (col // CHUNK) <= (row // CHUNK)
            for c in range(2):
                cs = slice(c * DA_HD, (c + 1) * DA_HD)
                s = jnp.dot(q_ref[rows, cs], kt_ref[kb, cs, :], preferred_element_type=F32)
                if masked:
                    s = jnp.where(mask, s, NEG_BIG)
                if first:
                    m_prev = jnp.full((ATT_ROWS, LANES), NEG_BIG, F32)
                else:
                    m_prev = m_ref[c, rows, :]
                m_new = jnp.maximum(m_prev, jnp.max(s, axis=-1, keepdims=True))
                ps = [jnp.exp2(s[:, j * LANES:(j + 1) * LANES] - m_new) for j in range(TK // LANES)]
                m_ref[c, rows, :] = m_new
                d_ref[slot, c, rows, :] = m_prev - m_new
                ps_ref[slot, c, rows, :] = functools.reduce(jnp.add, ps)
                p_ref[slot, c, rows, :] = jnp.concatenate(ps, axis=-1).astype(BF16)

    def values(kb, slot, first_row=0):
        v_blk = v_ref[pl.ds(pl.multiple_of(kb * TK, TK), TK), :]
        for r0 in range(first_row, TQ, PV_ROWS):
            rows = slice(r0, r0 + PV_ROWS)
            for c in range(2):
                pv = jnp.dot(p_ref[slot, c, rows, :], v_blk, preferred_element_type=F32)
                alpha = jnp.exp2(d_ref[slot, c, rows, :])
                l_ref[c, rows, :] = alpha * l_ref[c, rows, :] + ps_ref[slot, c, rows, :]
                scale = jnp.concatenate([alpha] * (2 * DA_HD // LANES), axis=-1)
                acc_ref[c, rows, :] = scale * acc_ref[c, rows, :] + pv

    @pl.when(qi == 0)
    def _():
        probs(0, 0, 0, first=True)

    @pl.when(qi > 0)
    def _():
        probs(0, 0, None, first=True)

    def body(t, carry):
        probs(2 * t + 1, 1, None)
        values(2 * t, 0)
        probs(2 * t + 2, 0, None)
        values(2 * t + 1, 1)
        return carry

    lax.fori_loop(0, qi - 1, body, 0)

    @pl.when(qi > 0)
    def _():
        probs(2 * qi - 1, 1, None)
        values(2 * qi - 2, 0)
        probs(2 * qi, 0, 0)
        values(2 * qi - 1, 1)

    lam = (jnp.exp(jnp.sum(lq1_ref[...] * lk1_ref[...], axis=-1, keepdims=True))
           - jnp.exp(jnp.sum(lq2_ref[...] * lk2_ref[...], axis=-1, keepdims=True))
           + lam_init)

    def finish(rows):
        inv0 = 1.0 / jnp.sum(l_ref[0, rows, :], axis=-1, keepdims=True)
        inv1 = lam / jnp.sum(l_ref[1, rows, :], axis=-1, keepdims=True)
        o = acc_ref[0, rows, :] * inv0 - acc_ref[1, rows, :] * inv1
        r = lax.rsqrt(jnp.mean(o * o, axis=-1, keepdims=True) + EPS) * (1.0 - lam_init)
        y = (o * r) * (sg_ref[...] * g_ref[rows, :].astype(F32))
        o_ref[rows, :] = y.astype(o_ref.dtype)

    probs(2 * qi + 1, 1, TK)
    values(2 * qi, 0)
    finish(slice(0, TK))
    values(2 * qi + 1, 1, first_row=TK)
    finish(slice(TK, TQ))


def _attention(q, kt, vb, gates, lq1, lk1, lq2, lk2, subln_g, lam_init, batch, seq):
    t = q.shape[0]
    nq = seq // TQ
    nk = seq // TK
    hw = 2 * DA_HD
    vec = pl.BlockSpec((1, DA_HD), lambda b, h, i: (0, 0))
    return pl.pallas_call(
        functools.partial(_attn_kernel, lam_init),
        out_shape=jax.ShapeDtypeStruct((t, D_B), BF16),
        grid=(batch, DA_HEADS, nq),
        in_specs=[pl.BlockSpec((TQ, hw), lambda b, h, i: (b * nq + i, h)),
                  pl.BlockSpec((nk, hw, TK), lambda b, h, i: (b, h, 0)),
                  pl.BlockSpec((seq, hw), lambda b, h, i: (b, h)),
                  pl.BlockSpec((TQ, hw), lambda b, h, i: (b * nq + i, DA_HEADS + h)),
                  vec, vec, vec, vec,
                  pl.BlockSpec((1, hw), lambda b, h, i: (0, 0))],
        out_specs=pl.BlockSpec((TQ, hw), lambda b, h, i: (b * nq + i, h)),
        scratch_shapes=[pltpu.VMEM((2, 2, TQ, TK), BF16),
                        pltpu.VMEM((2, 2, TQ, LANES), F32),
                        pltpu.VMEM((2, 2, TQ, LANES), F32),
                        pltpu.VMEM((2, TQ, hw), F32),
                        pltpu.VMEM((2, TQ, LANES), F32),
                        pltpu.VMEM((2, TQ, LANES), F32)],
        compiler_params=_params("arbitrary", "arbitrary", "arbitrary"),
        name="diff_attention",
    )(q, kt, vb, gates, lq1.reshape(1, DA_HD), lk1.reshape(1, DA_HD),
      lq2.reshape(1, DA_HD), lk2.reshape(1, DA_HD), subln_g.reshape(1, hw))


def _outproj_kernel(ya_ref, yb_ref, wa_ref, wb_ref, x_ref, *refs):
    for r0 in range(0, BM, OUT_PIECE):
        rows = slice(r0, r0 + OUT_PIECE)
        acc = jnp.dot(ya_ref[rows, :], wa_ref[...], preferred_element_type=F32)
        acc = acc + jnp.dot(yb_ref[rows, :], wb_ref[...], preferred_element_type=F32)
        x_new = x_ref[rows, :] + acc
        if len(refs) == 1:
            (o_ref,) = refs
        else:
            g_ref, o_ref, xg_ref, ssq_ref = refs
            xg_ref[rows, :] = (x_new * g_ref[...]).astype(xg_ref.dtype)
            sq = x_new * x_new
            ssq_ref[rows, :] = functools.reduce(
                jnp.add, [sq[:, k * LANES:(k + 1) * LANES] for k in range(BN // LANES)])
        o_ref[rows, :] = x_new


def _outproj(ya, yb, w, layer, x, next_gain=None):
    t, d = x.shape
    in_specs = [pl.BlockSpec((BM, D_A), lambda j, i: (i, 0)),
                pl.BlockSpec((BM, D_B), lambda j, i: (i, 0)),
                pl.BlockSpec((None, D_A, BN), lambda j, i: (layer, 0, j)),
                pl.BlockSpec((None, D_B, BN), lambda j, i: (layer, 1, j)),
                pl.BlockSpec((BM, BN), lambda j, i: (i, j))]
    out_shape = jax.ShapeDtypeStruct((t, d), F32)
    out_specs = pl.BlockSpec((BM, BN), lambda j, i: (i, j))
    args = (ya, yb, w, w, x)
    if next_gain is not None:
        in_specs.append(pl.BlockSpec((1, BN), lambda j, i: (0, j)))
        args += (next_gain.reshape(1, d),)
        out_shape = (out_shape, jax.ShapeDtypeStruct((t, d), BF16),
                     jax.ShapeDtypeStruct((d // BN, t, LANES), F32))
        out_specs = (out_specs, pl.BlockSpec((BM, BN), lambda j, i: (i, j)),
                     pl.BlockSpec((None, BM, LANES), lambda j, i: (j, i, 0)))
    return pl.pallas_call(
        _outproj_kernel,
        out_shape=out_shape,
        grid=(d // BN, t // BM),
        in_specs=in_specs,
        out_specs=out_specs,
        compiler_params=_params("arbitrary", "arbitrary"),
        name="outproj",
    )(*args)


def _row_scale_kernel(ssq_ref, o_ref):
    total = jnp.sum(functools.reduce(jnp.add, [ssq_ref[k] for k in range(ssq_ref.shape[0])]),
                    axis=-1, keepdims=True)
    o_ref[...] = jnp.broadcast_to(lax.rsqrt(total * (1.0 / D_MODEL) + EPS), o_ref.shape)


def _row_scale(ssq):
    nparts, t, _ = ssq.shape
    rows = 2048
    return pl.pallas_call(
        _row_scale_kernel,
        out_shape=jax.ShapeDtypeStruct((t, LANES), F32),
        grid=(t // rows,),
        in_specs=[pl.BlockSpec((nparts, rows, LANES), lambda i: (0, i, 0))],
        out_specs=pl.BlockSpec((rows, LANES), lambda i: (i, 0)),
        compiler_params=_params("arbitrary"),
        name="row_scale",
    )(ssq)


def kernel(x, positions, norm_g, w_in, ln_g, ln_b, sgu_w, sgu_b, lam_q1, lam_k1, lam_q2, lam_k2,
           subln_g, w_out, final_g):
    batch, seq, d = x.shape
    t = batch * seq
    x = x.reshape(t, d)
    rope_c, rope_s1, rope_s2 = _rope_tables(positions)
    rope = (rope_c, rope_s1, rope_s2)
    w_out = w_out.astype(BF16)
    nb = D_A // BN
    h = _rmsnorm(x, norm_g[0], BF16, "rmsnorm_in")
    row_scale = None
    for l in range(DEPTH):
        guv = _inproj(h, row_scale, w_in, l, lambda j: j, 2 * nb,
                      functools.partial(_elementwise_epilogue, _gelu), name="inproj_gelu")
        gates = _inproj(h, row_scale, w_in, l, lambda j: jnp.where(j < nb, 2 * nb + j, 5 * nb + j), 2 * nb,
                        functools.partial(_elementwise_epilogue, _silu), name="inproj_silu")
        q = _inproj(h, row_scale, w_in, l, lambda j: 3 * nb + j, nb,
                    functools.partial(_rope_epilogue, DA_HD ** -0.5 * math.log2(math.e), False),
                    extra=rope, name="inproj_q")
        kt = _inproj(h, row_scale, w_in, l, lambda j: 4 * nb + j, nb,
                     functools.partial(_rope_epilogue, 1.0, True),
                     extra=rope, transposed=True, piece=256, name="inproj_k")
        vb = _inproj(h, row_scale, w_in, l, lambda j: 5 * nb + j, nb,
                     functools.partial(_elementwise_epilogue, _identity), name="inproj_v")
        ya = _sgu(guv, gates, ln_g[l], ln_b[l], sgu_w[l], sgu_b[l])
        lam_init = 0.8 - 0.6 * math.exp(-0.3 * l)
        yb = _attention(q, kt, vb, gates, lam_q1[l], lam_k1[l], lam_q2[l], lam_k2[l], subln_g[l],
                        lam_init, batch, seq)
        if l + 1 < DEPTH:
            x, h, ssq = _outproj(ya, yb, w_out, l, x, next_gain=norm_g[l + 1])
            row_scale = _row_scale(ssq)
        else:
            x = _outproj(ya, yb, w_out, l, x)
    out = _rmsnorm(x, final_g, F32, "rmsnorm_out")
    return out.reshape(batch, seq, d)
```

```python
import functools
import math

import jax
import jax.numpy as jnp
from jax import lax
from jax.experimental import pallas as pl
from jax.experimental.pallas import tpu as pltpu

D_MODEL = 4096
DEPTH = 2
CHUNK = 64
D_A = D_MODEL // 2
D_B = D_MODEL - D_A
SGU_BLOCK = 128
SGU_GROUPS = 8
SGU_GW = D_A // SGU_GROUPS
DA_HEADS = 8
DA_HD = D_B // DA_HEADS // 2
ROT_DIM = DA_HD // 4
ROPE_THETA = 500000.0
EPS = 1e-5

LANES = 128
VMEM_LIMIT = 56 * 1024 * 1024
NEG_BIG = -1e30

BM = 1024
BN = 1024
OUT_PIECE = 256
W_CHUNK = 512
W_PIECES = D_MODEL // W_CHUNK
NORM_ROWS = 512
SGU_ROWS = 1024
TQ = 1024
TK = TQ // 2
ATT_ROWS = 128
PV_ROWS = 512

F32 = jnp.float32
BF16 = jnp.bfloat16


def _params(*sem):
    return pltpu.CompilerParams(dimension_semantics=sem, vmem_limit_bytes=VMEM_LIMIT)


def _rmsnorm_kernel(x_ref, g_ref, o_ref):
    x = x_ref[...]
    r = lax.rsqrt(jnp.mean(x * x, axis=-1, keepdims=True) + EPS)
    o_ref[...] = (x * r * g_ref[...]).astype(o_ref.dtype)


def _rmsnorm(x, g, out_dtype, name):
    t, d = x.shape
    return pl.pallas_call(
        _rmsnorm_kernel,
        out_shape=jax.ShapeDtypeStruct((t, d), out_dtype),
        grid=(t // NORM_ROWS,),
        in_specs=[pl.BlockSpec((NORM_ROWS, d), lambda i: (i, 0)),
                  pl.BlockSpec((1, d), lambda i: (0, 0))],
        out_specs=pl.BlockSpec((NORM_ROWS, d), lambda i: (i, 0)),
        compiler_params=_params("arbitrary"),
        name=name,
    )(x, g.reshape(1, d))


def _rope_table_kernel(pos_ref, freq_ref, c_ref, s1_ref, s2_ref):
    ang = pos_ref[...] * freq_ref[...]
    cos = jnp.cos(ang)
    sin = jnp.sin(ang)
    lane = lax.broadcasted_iota(jnp.int32, ang.shape, 1)
    half = ROT_DIM // 2
    c_ref[...] = jnp.where(lane < ROT_DIM, cos, 1.0)
    s1_ref[...] = jnp.where(lane < half, -sin, 0.0)
    s2_ref[...] = jnp.where((lane >= half) & (lane < ROT_DIM), sin, 0.0)


def _rope_tables(positions):
    t = positions.size
    rows = 2048
    inv_freq = ROPE_THETA ** (-jnp.arange(0, ROT_DIM, 2, dtype=F32) / ROT_DIM)
    freq_row = jnp.zeros((1, LANES), F32).at[0, :ROT_DIM].set(jnp.tile(inv_freq, 2))
    pos = positions.astype(F32).reshape(t, 1)
    tab = jax.ShapeDtypeStruct((t, LANES), F32)
    spec = pl.BlockSpec((rows, LANES), lambda i: (i, 0))
    return pl.pallas_call(
        _rope_table_kernel,
        out_shape=(tab, tab, tab),
        grid=(t // rows,),
        in_specs=[pl.BlockSpec((rows, 1), lambda i: (i, 0)),
                  pl.BlockSpec((1, LANES), lambda i: (0, 0))],
        out_specs=(spec, spec, spec),
        compiler_params=_params("arbitrary"),
        name="rope_tables",
    )(pos, freq_row)


def _gelu(a):
    return 0.5 * a * (1.0 + lax.erf(a * (1.0 / math.sqrt(2.0))))


def _silu(a):
    return a / (1.0 + jnp.exp(-a))


def _identity(a):
    return a


def _stage_weights(w_hbm, layer, col_block, n_blocks, wbf_ref, stage_ref, sem_ref):
    j = pl.program_id(0)
    i = pl.program_id(1)

    def piece(block, c, slot):
        col = pl.multiple_of(col_block(block) * BN, BN)
        row = pl.multiple_of(c * W_CHUNK, W_CHUNK)
        return pltpu.make_async_copy(w_hbm.at[layer, pl.ds(row, W_CHUNK), pl.ds(col, BN)],
                                     stage_ref.at[slot], sem_ref.at[slot])

    def convert(block, c, slot):
        row = pl.multiple_of(c * W_CHUNK, W_CHUNK)
        wbf_ref[block % 2, pl.ds(row, W_CHUNK), :] = stage_ref[slot].astype(BF16)

    @pl.when((j == 0) & (i == 0))
    def _():
        piece(0, 0, 0).start()
        for c in range(W_PIECES):
            if c + 1 < W_PIECES:
                piece(0, c + 1, (c + 1) % 2).start()
            piece(0, c, c % 2).wait()
            convert(0, c, c % 2)

    @pl.when(j + 1 < n_blocks)
    def _():
        @pl.when((i >= 1) & (i <= W_PIECES))
        def _():
            piece(j + 1, i - 1, (i - 1) % 2).wait()
            convert(j + 1, i - 1, (i - 1) % 2)

        @pl.when(i < W_PIECES)
        def _():
            piece(j + 1, i, i % 2).start()


def _inproj_kernel(epilogue, layer, col_block, n_blocks, row_scaled, piece, x_ref, w_hbm, *refs):
    *extra, o_ref, wbf_ref, stage_ref, sem_ref = refs
    _stage_weights(w_hbm, layer, col_block, n_blocks, wbf_ref, stage_ref, sem_ref)
    w = wbf_ref[pl.program_id(0) % 2]
    if row_scaled:
        r_ref, *extra = extra
    for r0 in range(0, BM, piece):
        rows = slice(r0, r0 + piece)
        acc = jnp.dot(x_ref[rows, :], w, preferred_element_type=F32)
        if row_scaled:
            acc = acc * jnp.concatenate([r_ref[rows, :]] * (BN // LANES), axis=-1)
        epilogue(acc, rows, *extra, o_ref)


def _elementwise_epilogue(fn, acc, rows, o_ref):
    o_ref[rows, :] = fn(acc).astype(o_ref.dtype)


def _rope_epilogue(scale, transposed, acc, rows, c_ref, s1_ref, s2_ref, o_ref):
    c = c_ref[rows, :] * scale
    s1 = s1_ref[rows, :] * scale
    s2 = s2_ref[rows, :] * scale
    half = ROT_DIM // 2
    for g in range(BN // LANES):
        cols = slice(g * LANES, (g + 1) * LANES)
        a = acc[:, cols]
        up = pltpu.roll(a, LANES - half, 1)
        down = pltpu.roll(a, half, 1)
        out = a * c + up * s1 + down * s2
        if transposed:
            lanes = slice(rows.start % TK, rows.start % TK + rows.stop - rows.start)
            o_ref[rows.start // TK, cols, lanes] = out.T.astype(o_ref.dtype)
        else:
            o_ref[rows, cols] = out.astype(o_ref.dtype)


def _inproj(h, row_scale, w, layer, col_block, n_blocks, epilogue, extra=(), transposed=False,
            piece=128, name=None):
    t, d = h.shape
    assert d == W_PIECES * W_CHUNK and t // BM > W_PIECES
    if row_scale is not None:
        extra = (row_scale,) + tuple(extra)
    extra_specs = [pl.BlockSpec((BM, LANES), lambda j, i: (i, 0)) for _ in extra]
    if transposed:
        out_shape = jax.ShapeDtypeStruct((t // TK, n_blocks * BN, TK), BF16)
        out_spec = pl.BlockSpec((BM // TK, BN, TK), lambda j, i: (i, j, 0))
    else:
        out_shape = jax.ShapeDtypeStruct((t, n_blocks * BN), BF16)
        out_spec = pl.BlockSpec((BM, BN), lambda j, i: (i, j))
    return pl.pallas_call(
        functools.partial(_inproj_kernel, epilogue, layer, col_block, n_blocks, row_scale is not None, piece),
        out_shape=out_shape,
        grid=(n_blocks, t // BM),
        in_specs=[pl.BlockSpec((BM, d), lambda j, i: (i, 0)),
                  pl.BlockSpec(memory_space=pl.ANY)] + extra_specs,
        out_specs=out_spec,
        scratch_shapes=[pltpu.VMEM((2, d, BN), BF16),
                        pltpu.VMEM((2, W_CHUNK, BN), F32),
                        pltpu.SemaphoreType.DMA((2,))],
        compiler_params=_params("arbitrary", "arbitrary"),
        name=name,
    )(h, w, *extra)


def _sgu_kernel(gu_ref, gv_ref, sg_ref, lng_ref, lnb_ref, w_ref, b_ref, o_ref):
    v = gv_ref[...].astype(F32)
    mu = jnp.mean(v, axis=-1, keepdims=True)
    xc = v - mu
    var = jnp.mean(xc * xc, axis=-1, keepdims=True)
    vn = (xc * lax.rsqrt(var + EPS) * lng_ref[...] + lnb_ref[...]).astype(BF16)
    row = lax.broadcasted_iota(jnp.int32, (SGU_BLOCK, SGU_BLOCK), 0)
    col = lax.broadcasted_iota(jnp.int32, (SGU_BLOCK, SGU_BLOCK), 1)
    mask = (col // CHUNK) <= (row // CHUNK)
    for g in range(SGU_GROUPS):
        wg = jnp.where(mask, w_ref[g], 0.0).astype(BF16)
        bg = b_ref[:, g:g + 1]
        cs = slice(g * SGU_GW, (g + 1) * SGU_GW)
        for r in range(SGU_ROWS // SGU_BLOCK):
            rs = slice(r * SGU_BLOCK, (r + 1) * SGU_BLOCK)
            mix = jnp.dot(wg, vn[rs, cs], preferred_element_type=F32) + bg
            o_ref[rs, cs] = mix.astype(BF16) * (gu_ref[rs, cs] * sg_ref[rs, cs])


def _sgu(guv, sgate, ln_g, ln_b, sgu_w, sgu_b):
    t = guv.shape[0]
    return pl.pallas_call(
        _sgu_kernel,
        out_shape=jax.ShapeDtypeStruct((t, D_A), BF16),
        grid=(t // SGU_ROWS,),
        in_specs=[pl.BlockSpec((SGU_ROWS, D_A), lambda i: (i, 0)),
                  pl.BlockSpec((SGU_ROWS, D_A), lambda i: (i, 1)),
                  pl.BlockSpec((SGU_ROWS, D_A), lambda i: (i, 0)),
                  pl.BlockSpec((1, D_A), lambda i: (0, 0)),
                  pl.BlockSpec((1, D_A), lambda i: (0, 0)),
                  pl.BlockSpec((SGU_GROUPS, SGU_BLOCK, SGU_BLOCK), lambda i: (0, 0, 0)),
                  pl.BlockSpec((SGU_BLOCK, SGU_GROUPS), lambda i: (0, 0))],
        out_specs=pl.BlockSpec((SGU_ROWS, D_A), lambda i: (i, 0)),
        compiler_params=_params("arbitrary"),
        name="sgu",
    )(guv, guv, sgate, ln_g.reshape(1, D_A), ln_b.reshape(1, D_A), sgu_w, sgu_b.T)


def _attn_kernel(lam_init, q_ref, kt_ref, v_ref, g_ref, lq1_ref, lk1_ref, lq2_ref, lk2_ref, sg_ref,
                 o_ref, p_ref, d_ref, ps_ref, acc_ref, m_ref, l_ref):
    qi = pl.program_id(2)

    l_ref[...] = jnp.zeros(l_ref.shape, F32)
    acc_ref[...] = jnp.zeros(acc_ref.shape, F32)

    def probs(kb, slot, col0, first=False):
        for r0 in range(0, TQ, ATT_ROWS):
            if col0 is not None and r0 + ATT_ROWS <= col0:
                continue
            rows = slice(r0, r0 + ATT_ROWS)
            masked = col0 is not None and col0 + TK > r0
            if masked:
                row = lax.broadcasted_iota(jnp.int32, (ATT_ROWS, TK), 0) + r0
                col = lax.broadcasted_iota(jnp.int32, (ATT_ROWS, TK), 1) + col0
                mask = (col // CHUNK) <= (row // CHUNK)
            for c in range(2):
                cs = slice(c * DA_HD, (c + 1) * DA_HD)
                s = jnp.dot(q_ref[rows, cs], kt_ref[kb, cs, :], preferred_element_type=F32)
                if masked:
                    s = jnp.where(mask, s, NEG_BIG)
                if first:
                    m_prev = jnp.full((ATT_ROWS, LANES), NEG_BIG, F32)
                else:
                    m_prev = m_ref[c, rows, :]
                m_new = jnp.maximum(m_prev, jnp.max(s, axis=-1, keepdims=True))
                ps = [jnp.exp2(s[:, j * LANES:(j + 1) * LANES] - m_new) for j in range(TK // LANES)]
                m_ref[c, rows, :] = m_new
                d_ref[slot, c, rows, :] = m_prev - m_new
                ps_ref[slot, c, rows, :] = functools.reduce(jnp.add, ps)
                p_ref[slot, c, rows, :] = jnp.concatenate(ps, axis=-1).astype(BF16)

    def values(kb, slot, first_row=0):
        v_blk = v_ref[pl.ds(pl.multiple_of(kb * TK, TK), TK), :]
        for r0 in range(first_row, TQ, PV_ROWS):
            rows = slice(r0, r0 + PV_ROWS)
            for c in range(2):
                pv = jnp.dot(p_ref[slot, c, rows, :], v_blk, preferred_element_type=F32)
                alpha = jnp.exp2(d_ref[slot, c, rows, :])
                l_ref[c, rows, :] = alpha * l_ref[c, rows, :] + ps_ref[slot, c, rows, :]
                scale = jnp.concatenate([alpha] * (2 * DA_HD // LANES), axis=-1)
                acc_ref[c, rows, :] = scale * acc_ref[c, rows, :] + pv

    @pl.when(qi == 0)
    def _():
        probs(0, 0, 0, first=True)

    @pl.when(qi > 0)
    def _():
        probs(0, 0, None, first=True)

    def body(t, carry):
        probs(2 * t + 1, 1, None)
        values(2 * t, 0)
        probs(2 * t + 2, 0, None)
        values(2 * t + 1, 1)
        return carry

    lax.fori_loop(0, qi - 1, body, 0)

    @pl.when(qi > 0)
    def _():
        probs(2 * qi - 1, 1, None)
        values(2 * qi - 2, 0)
        probs(2 * qi, 0, 0)
        values(2 * qi - 1, 1)

    lam = (jnp.exp(jnp.sum(lq1_ref[...] * lk1_ref[...], axis=-1, keepdims=True))
           - jnp.exp(jnp.sum(lq2_ref[...] * lk2_ref[...], axis=-1, keepdims=True))
           + lam_init)

    def finish(rows):
        inv0 = 1.0 / jnp.sum(l_ref[0, rows, :], axis=-1, keepdims=True)
        inv1 = lam / jnp.sum(l_ref[1, rows, :], axis=-1, keepdims=True)
        o = acc_ref[0, rows, :] * inv0 - acc_ref[1, rows, :] * inv1
        r = lax.rsqrt(jnp.mean(o * o, axis=-1, keepdims=True) + EPS) * (1.0 - lam_init)
        y = (o * r) * (sg_ref[...] * g_ref[rows, :].astype(F32))
        o_ref[rows, :] = y.astype(o_ref.dtype)

    probs(2 * qi + 1, 1, TK)
    values(2 * qi, 0)
    finish(slice(0, TK))
    values(2 * qi + 1, 1, first_row=TK)
    finish(slice(TK, TQ))


def _attention(q, kt, vb, gates, lq1, lk1, lq2, lk2, subln_g, lam_init, batch, seq):
    t = q.shape[0]
    nq = seq // TQ
    nk = seq // TK
    hw = 2 * DA_HD
    vec = pl.BlockSpec((1, DA_HD), lambda b, h, i: (0, 0))
    return pl.pallas_call(
        functools.partial(_attn_kernel, lam_init),
        out_shape=jax.ShapeDtypeStruct((t, D_B), BF16),
        grid=(batch, DA_HEADS, nq),
        in_specs=[pl.BlockSpec((TQ, hw), lambda b, h, i: (b * nq + i, h)),
                  pl.BlockSpec((nk, hw, TK), lambda b, h, i: (b, h, 0)),
                  pl.BlockSpec((seq, hw), lambda b, h, i: (b, h)),
                  pl.BlockSpec((TQ, hw), lambda b, h, i: (b * nq + i, DA_HEADS + h)),
                  vec, vec, vec, vec,
                  pl.BlockSpec((1, hw), lambda b, h, i: (0, 0))],
        out_specs=pl.BlockSpec((TQ, hw), lambda b, h, i: (b * nq + i, h)),
        scratch_shapes=[pltpu.VMEM((2, 2, TQ, TK), BF16),
                        pltpu.VMEM((2, 2, TQ, LANES), F32),
                        pltpu.VMEM((2, 2, TQ, LANES), F32),
                        pltpu.VMEM((2, TQ, hw), F32),
                        pltpu.VMEM((2, TQ, LANES), F32),
                        pltpu.VMEM((2, TQ, LANES), F32)],
        compiler_params=_params("arbitrary", "arbitrary", "arbitrary"),
        name="diff_attention",
    )(q, kt, vb, gates, lq1.reshape(1, DA_HD), lk1.reshape(1, DA_HD),
      lq2.reshape(1, DA_HD), lk2.reshape(1, DA_HD), subln_g.reshape(1, hw))


def _outproj_kernel(ya_ref, yb_ref, wa_ref, wb_ref, x_ref, *refs):
    for r0 in range(0, BM, OUT_PIECE):
        rows = slice(r0, r0 + OUT_PIECE)
        acc = jnp.dot(ya_ref[rows, :], wa_ref[...], preferred_element_type=F32)
        acc = acc + jnp.dot(yb_ref[rows, :], wb_ref[...], preferred_element_type=F32)
        x_new = x_ref[rows, :] + acc
        if len(refs) == 1:
            (o_ref,) = refs
        else:
            g_ref, o_ref, xg_ref, ssq_ref = refs
            xg_ref[rows, :] = (x_new * g_ref[...]).astype(xg_ref.dtype)
            sq = x_new * x_new
            ssq_ref[rows, :] = functools.reduce(
                jnp.add, [sq[:, k * LANES:(k + 1) * LANES] for k in range(BN // LANES)])
        o_ref[rows, :] = x_new


def _outproj(ya, yb, w, layer, x, next_gain=None):
    t, d = x.shape
    in_specs = [pl.BlockSpec((BM, D_A), lambda j, i: (i, 0)),
                pl.BlockSpec((BM, D_B), lambda j, i: (i, 0)),
                pl.BlockSpec((None, D_A, BN), lambda j, i: (layer, 0, j)),
                pl.BlockSpec((None, D_B, BN), lambda j, i: (layer, 1, j)),
                pl.BlockSpec((BM, BN), lambda j, i: (i, j))]
    out_shape = jax.ShapeDtypeStruct((t, d), F32)
    out_specs = pl.BlockSpec((BM, BN), lambda j, i: (i, j))
    args = (ya, yb, w, w, x)
    if next_gain is not None:
        in_specs.append(pl.BlockSpec((1, BN), lambda j, i: (0, j)))
        args += (next_gain.reshape(1, d),)
        out_shape = (out_shape, jax.ShapeDtypeStruct((t, d), BF16),
                     jax.ShapeDtypeStruct((d // BN, t, LANES), F32))
        out_specs = (out_specs, pl.BlockSpec((BM, BN), lambda j, i: (i, j)),
                     pl.BlockSpec((None, BM, LANES), lambda j, i: (j, i, 0)))
    return pl.pallas_call(
        _outproj_kernel,
        out_shape=out_shape,
        grid=(d // BN, t // BM),
        in_specs=in_specs,
        out_specs=out_specs,
        compiler_params=_params("arbitrary", "arbitrary"),
        name="outproj",
    )(*args)


def _row_scale_kernel(ssq_ref, o_ref):
    total = jnp.sum(functools.reduce(jnp.add, [ssq_ref[k] for k in range(ssq_ref.shape[0])]),
                    axis=-1, keepdims=True)
    o_ref[...] = jnp.broadcast_to(lax.rsqrt(total * (1.0 / D_MODEL) + EPS), o_ref.shape)


def _row_scale(ssq):
    nparts, t, _ = ssq.shape
    rows = 2048
    return pl.pallas_call(
        _row_scale_kernel,
        out_shape=jax.ShapeDtypeStruct((t, LANES), F32),
        grid=(t // rows,),
        in_specs=[pl.BlockSpec((nparts, rows, LANES), lambda i: (0, i, 0))],
        out_specs=pl.BlockSpec((rows, LANES), lambda i: (i, 0)),
        compiler_params=_params("arbitrary"),
        name="row_scale",
    )(ssq)


def kernel(x, positions, norm_g, w_in, ln_g, ln_b, sgu_w, sgu_b, lam_q1, lam_k1, lam_q2, lam_k2,
           subln_g, w_out, final_g):
    batch, seq, d = x.shape
    t = batch * seq
    x = x.reshape(t, d)
    rope_c, rope_s1, rope_s2 = _rope_tables(positions)
    rope = (rope_c, rope_s1, rope_s2)
    w_out = w_out.astype(BF16)
    nb = D_A // BN
    h = _rmsnorm(x, norm_g[0], BF16, "rmsnorm_in")
    row_scale = None
    for l in range(DEPTH):
        guv = _inproj(h, row_scale, w_in, l, lambda j: j, 2 * nb,
                      functools.partial(_elementwise_epilogue, _gelu), name="inproj_gelu")
        gates = _inproj(h, row_scale, w_in, l, lambda j: jnp.where(j < nb, 2 * nb + j, 5 * nb + j), 2 * nb,
                        functools.partial(_elementwise_epilogue, _silu), name="inproj_silu")
        q = _inproj(h, row_scale, w_in, l, lambda j: 3 * nb + j, nb,
                    functools.partial(_rope_epilogue, DA_HD ** -0.5 * math.log2(math.e), False),
                    extra=rope, name="inproj_q")
        kt = _inproj(h, row_scale, w_in, l, lambda j: 4 * nb + j, nb,
                     functools.partial(_rope_epilogue, 1.0, True),
                     extra=rope, transposed=True, piece=256, name="inproj_k")
        vb = _inproj(h, row_scale, w_in, l, lambda j: 5 * nb + j, nb,
                     functools.partial(_elementwise_epilogue, _identity), name="inproj_v")
        ya = _sgu(guv, gates, ln_g[l], ln_b[l], sgu_w[l], sgu_b[l])
        lam_init = 0.8 - 0.6 * math.exp(-0.3 * l)
        yb = _attention(q, kt, vb, gates, lam_q1[l], lam_k1[l], lam_q2[l], lam_k2[l], subln_g[l],
                        lam_init, batch, seq)
        if l + 1 < DEPTH:
            x, h, ssq = _outproj(ya, yb, w_out, l, x, next_gain=norm_g[l + 1])
            row_scale = _row_scale(ssq)
        else:
            x = _outproj(ya, yb, w_out, l, x)
    out = _rmsnorm(x, final_g, F32, "rmsnorm_out")
    return out.reshape(batch, seq, d)
```

```python
import functools
import math

import jax
import jax.numpy as jnp
from jax import lax
from jax.experimental import pallas as pl
from jax.experimental.pallas import tpu as pltpu

D_MODEL = 4096
DEPTH = 2
CHUNK = 64
D_A = D_MODEL // 2
D_B = D_MODEL - D_A
SGU_BLOCK = 128
SGU_GROUPS = 8
SGU_GW = D_A // SGU_GROUPS
DA_HEADS = 8
DA_HD = D_B // DA_HEADS // 2
ROT_DIM = DA_HD // 4
ROPE_THETA = 500000.0
EPS = 1e-5

LANES = 128
VMEM_LIMIT = 56 * 1024 * 1024
NEG_BIG = -1e30

BM = 1024
BN = 1024
OUT_PIECE = 256
W_CHUNK = 512
W_PIECES = D_MODEL // W_CHUNK
NORM_ROWS = 512
SGU_ROWS = 512
TQ = 1024
TK = TQ // 2
ATT_ROWS = 128
PV_ROWS = 512

F32 = jnp.float32
BF16 = jnp.bfloat16


def _params(*sem):
    return pltpu.CompilerParams(dimension_semantics=sem, vmem_limit_bytes=VMEM_LIMIT)


def _rmsnorm_kernel(x_ref, g_ref, o_ref):
    x = x_ref[...]
    r = lax.rsqrt(jnp.mean(x * x, axis=-1, keepdims=True) + EPS)
    o_ref[...] = (x * r * g_ref[...]).astype(o_ref.dtype)


def _rmsnorm(x, g, out_dtype, name):
    t, d = x.shape
    return pl.pallas_call(
        _rmsnorm_kernel,
        out_shape=jax.ShapeDtypeStruct((t, d), out_dtype),
        grid=(t // NORM_ROWS,),
        in_specs=[pl.BlockSpec((NORM_ROWS, d), lambda i: (i, 0)),
                  pl.BlockSpec((1, d), lambda i: (0, 0))],
        out_specs=pl.BlockSpec((NORM_ROWS, d), lambda i: (i, 0)),
        compiler_params=_params("arbitrary"),
        name=name,
    )(x, g.reshape(1, d))


def _rope_table_kernel(pos_ref, freq_ref, c_ref, s1_ref, s2_ref):
    ang = pos_ref[...] * freq_ref[...]
    cos = jnp.cos(ang)
    sin = jnp.sin(ang)
    lane = lax.broadcasted_iota(jnp.int32, ang.shape, 1)
    half = ROT_DIM // 2
    c_ref[...] = jnp.where(lane < ROT_DIM, cos, 1.0)
    s1_ref[...] = jnp.where(lane < half, -sin, 0.0)
    s2_ref[...] = jnp.where((lane >= half) & (lane < ROT_DIM), sin, 0.0)


def _rope_tables(positions):
    t = positions.size
    rows = 2048
    inv_freq = ROPE_THETA ** (-jnp.arange(0, ROT_DIM, 2, dtype=F32) / ROT_DIM)
    freq_row = jnp.zeros((1, LANES), F32).at[0, :ROT_DIM].set(jnp.tile(inv_freq, 2))
    pos = positions.astype(F32).reshape(t, 1)
    tab = jax.ShapeDtypeStruct((t, LANES), F32)
    spec = pl.BlockSpec((rows, LANES), lambda i: (i, 0))
    return pl.pallas_call(
        _rope_table_kernel,
        out_shape=(tab, tab, tab),
        grid=(t // rows,),
        in_specs=[pl.BlockSpec((rows, 1), lambda i: (i, 0)),
                  pl.BlockSpec((1, LANES), lambda i: (0, 0))],
        out_specs=(spec, spec, spec),
        compiler_params=_params("arbitrary"),
        name="rope_tables",
    )(pos, freq_row)


def _gelu(a):
    return 0.5 * a * (1.0 + lax.erf(a * (1.0 / math.sqrt(2.0))))


def _silu(a):
    return a / (1.0 + jnp.exp(-a))


def _identity(a):
    return a


def _stage_weights(w_hbm, layer, col_block, n_blocks, wbf_ref, stage_ref, sem_ref, part):
    j = pl.program_id(0)
    i = pl.program_id(1)

    def piece(block, c, slot):
        col = pl.multiple_of(col_block(block) * BN, BN)
        row = pl.multiple_of(c * W_CHUNK, W_CHUNK)
        return pltpu.make_async_copy(w_hbm.at[layer, pl.ds(row, W_CHUNK), pl.ds(col, BN)],
                                     stage_ref.at[slot], sem_ref.at[slot])

    def convert(block, c, slot):
        row = pl.multiple_of(c * W_CHUNK, W_CHUNK)
        wbf_ref[block % 2, pl.ds(row, W_CHUNK), :] = stage_ref[slot].astype(BF16)

    if part == "first":
        @pl.when((j == 0) & (i == 0))
        def _():
            piece(0, 0, 0).start()
            for c in range(W_PIECES):
                if c + 1 < W_PIECES:
                    piece(0, c + 1, (c + 1) % 2).start()
                piece(0, c, c % 2).wait()
                convert(0, c, c % 2)
        return

    @pl.when(j + 1 < n_blocks)
    def _():
        @pl.when((i >= 1) & (i <= W_PIECES))
        def _():
            piece(j + 1, i - 1, (i - 1) % 2).wait()
            convert(j + 1, i - 1, (i - 1) % 2)

        @pl.when(i < W_PIECES)
        def _():
            piece(j + 1, i, i % 2).start()


def _inproj_kernel(epilogue, layer, col_block, n_blocks, row_scaled, piece, x_ref, w_hbm, *refs):
    *extra, o_ref, wbf_ref, stage_ref, sem_ref = refs
    _stage_weights(w_hbm, layer, col_block, n_blocks, wbf_ref, stage_ref, sem_ref, "first")
    w = wbf_ref[pl.program_id(0) % 2]
    if row_scaled:
        r_ref, *extra = extra
    for r0 in range(0, BM, piece):
        rows = slice(r0, r0 + piece)
        acc = jnp.dot(x_ref[rows, :], w, preferred_element_type=F32)
        if row_scaled:
            acc = acc * jnp.concatenate([r_ref[rows, :]] * (BN // LANES), axis=-1)
        epilogue(acc, rows, *extra, o_ref)
    _stage_weights(w_hbm, layer, col_block, n_blocks, wbf_ref, stage_ref, sem_ref, "next")


def _elementwise_epilogue(fn, acc, rows, o_ref):
    o_ref[rows, :] = fn(acc).astype(o_ref.dtype)


def _rope_epilogue(scale, transposed, acc, rows, c_ref, s1_ref, s2_ref, o_ref):
    c = c_ref[rows, :] * scale
    s1 = s1_ref[rows, :] * scale
    s2 = s2_ref[rows, :] * scale
    half = ROT_DIM // 2
    for g in range(BN // LANES):
        cols = slice(g * LANES, (g + 1) * LANES)
        a = acc[:, cols]
        up = pltpu.roll(a, LANES - half, 1)
        down = pltpu.roll(a, half, 1)
        out = a * c + up * s1 + down * s2
        if transposed:
            lanes = slice(rows.start % TK, rows.start % TK + rows.stop - rows.start)
            o_ref[rows.start // TK, cols, lanes] = out.T.astype(o_ref.dtype)
        else:
            o_ref[rows, cols] = out.astype(o_ref.dtype)


def _inproj(h, row_scale, w, layer, col_block, n_blocks, epilogue, extra=(), transposed=False,
            piece=128, name=None):
    t, d = h.shape
    assert d == W_PIECES * W_CHUNK and t // BM > W_PIECES
    if row_scale is not None:
        extra = (row_scale,) + tuple(extra)
    extra_specs = [pl.BlockSpec((BM, LANES), lambda j, i: (i, 0)) for _ in extra]
    if transposed:
        out_shape = jax.ShapeDtypeStruct((t // TK, n_blocks * BN, TK), BF16)
        out_spec = pl.BlockSpec((BM // TK, BN, TK), lambda j, i: (i, j, 0))
    else:
        out_shape = jax.ShapeDtypeStruct((t, n_blocks * BN), BF16)
        out_spec = pl.BlockSpec((BM, BN), lambda j, i: (i, j))
    return pl.pallas_call(
        functools.partial(_inproj_kernel, epilogue, layer, col_block, n_blocks, row_scale is not None, piece),
        out_shape=out_shape,
        grid=(n_blocks, t // BM),
        in_specs=[pl.BlockSpec((BM, d), lambda j, i: (i, 0)),
                  pl.BlockSpec(memory_space=pl.ANY)] + extra_specs,
        out_specs=out_spec,
        scratch_shapes=[pltpu.VMEM((2, d, BN), BF16),
                        pltpu.VMEM((2, W_CHUNK, BN), F32),
                        pltpu.SemaphoreType.DMA((2,))],
        compiler_params=_params("arbitrary", "arbitrary"),
        name=name,
    )(h, w, *extra)


def _sgu_kernel(gu_ref, gv_ref, sg_ref, lng_ref, lnb_ref, w_ref, b_ref, o_ref):
    v = gv_ref[...].astype(F32)
    mu = jnp.mean(v, axis=-1, keepdims=True)
    xc = v - mu
    var = jnp.mean(xc * xc, axis=-1, keepdims=True)
    vn = (xc * lax.rsqrt(var + EPS) * lng_ref[...] + lnb_ref[...]).astype(BF16)
    row = lax.broadcasted_iota(jnp.int32, (SGU_BLOCK, SGU_BLOCK), 0)
    col = lax.broadcasted_iota(jnp.int32, (SGU_BLOCK, SGU_BLOCK), 1)
    mask = (col // CHUNK) <= (row // CHUNK)
    for g in range(SGU_GROUPS):
        wg = jnp.where(mask, w_ref[g], 0.0).astype(BF16)
        bg = b_ref[:, g:g + 1]
        cs = slice(g * SGU_GW, (g + 1) * SGU_GW)
        for r in range(SGU_ROWS // SGU_BLOCK):
            rs = slice(r * SGU_BLOCK, (r + 1) * SGU_BLOCK)
            mix = jnp.dot(wg, vn[rs, cs], preferred_element_type=F32) + bg
            y = gu_ref[rs, cs].astype(F32) * mix * sg_ref[rs, cs].astype(F32)
            o_ref[rs, cs] = y.astype(o_ref.dtype)


def _sgu(guv, sgate, ln_g, ln_b, sgu_w, sgu_b):
    t = guv.shape[0]
    return pl.pallas_call(
        _sgu_kernel,
        out_shape=jax.ShapeDtypeStruct((t, D_A), BF16),
        grid=(t // SGU_ROWS,),
        in_specs=[pl.BlockSpec((SGU_ROWS, D_A), lambda i: (i, 0)),
                  pl.BlockSpec((SGU_ROWS, D_A), lambda i: (i, 1)),
                  pl.BlockSpec((SGU_ROWS, D_A), lambda i: (i, 0)),
                  pl.BlockSpec((1, D_A), lambda i: (0, 0)),
                  pl.BlockSpec((1, D_A), lambda i: (0, 0)),
                  pl.BlockSpec((SGU_GROUPS, SGU_BLOCK, SGU_BLOCK), lambda i: (0, 0, 0)),
                  pl.BlockSpec((SGU_BLOCK, SGU_GROUPS), lambda i: (0, 0))],
        out_specs=pl.BlockSpec((SGU_ROWS, D_A), lambda i: (i, 0)),
        compiler_params=_params("arbitrary"),
        name="sgu",
    )(guv, guv, sgate, ln_g.reshape(1, D_A), ln_b.reshape(1, D_A), sgu_w, sgu_b.T)


def _attn_kernel(lam_init, q_ref, kt_ref, v_ref, g_ref, lq1_ref, lk1_ref, lq2_ref, lk2_ref, sg_ref,
                 o_ref, p_ref, d_ref, ps_ref, acc_ref, m_ref, l_ref):
    qi = pl.program_id(2)

    l_ref[...] = jnp.zeros(l_ref.shape, F32)
    acc_ref[...] = jnp.zeros(acc_ref.shape, F32)

    def probs(kb, slot, col0, first=False):
        for r0 in range(0, TQ, ATT_ROWS):
            if col0 is not None and r0 + ATT_ROWS <= col0:
                continue
            rows = slice(r0, r0 + ATT_ROWS)
            masked = col0 is not None and col0 + TK > r0
            if masked:
                row = lax.broadcasted_iota(jnp.int32, (ATT_ROWS, TK), 0) + r0
                col = lax.broadcasted_iota(jnp.int32, (ATT_ROWS, TK), 1) + col0
                mask = (col // CHUNK) <= (row // CHUNK)
            for c in range(2):
                cs = slice(c * DA_HD, (c + 1) * DA_HD)
                s = jnp.dot(q_ref[rows, cs], kt_ref[kb, cs, :], preferred_element_type=F32)
                if masked:
                    s = jnp.where(mask, s, NEG_BIG)
                if first:
                    m_prev = jnp.full((ATT_ROWS, LANES), NEG_BIG, F32)
                else:
                    m_prev = m_ref[c, rows, :]
                m_new = jnp.maximum(m_prev, jnp.max(s, axis=-1, keepdims=True))
                ps = [jnp.exp2(s[:, j * LANES:(j + 1) * LANES] - m_new) for j in range(TK // LANES)]
                m_ref[c, rows, :] = m_new
                d_ref[slot, c, rows, :] = m_prev - m_new
                ps_ref[slot, c, rows, :] = functools.reduce(jnp.add, ps)
                p_ref[slot, c, rows, :] = jnp.concatenate(ps, axis=-1).astype(BF16)

    def values(kb, slot, first_row=0):
        v_blk = v_ref[pl.ds(pl.multiple_of(kb * TK, TK), TK), :]
        for r0 in range(first_row, TQ, PV_ROWS):
            rows = slice(r0, r0 + PV_ROWS)
            for c in range(2):
                pv = jnp.dot(p_ref[slot, c, rows, :], v_blk, preferred_element_type=F32)
                alpha = jnp.exp2(d_ref[slot, c, rows, :])
                l_ref[c, rows, :] = alpha * l_ref[c, rows, :] + ps_ref[slot, c, rows, :]
                scale = jnp.concatenate([alpha] * (2 * DA_HD // LANES), axis=-1)
                acc_ref[c, rows, :] = scale * acc_ref[c, rows, :] + pv

    @pl.when(qi == 0)
    def _():
        probs(0, 0, 0, first=True)

    @pl.when(qi > 0)
    def _():
        probs(0, 0, None, first=True)

    def body(t, carry):
        probs(2 * t + 1, 1, None)
        values(2 * t, 0)
        probs(2 * t + 2, 0, None)
        values(2 * t + 1, 1)
        return carry

    lax.fori_loop(0, qi - 1, body, 0)

    @pl.when(qi > 0)
    def _():
        probs(2 * qi - 1, 1, None)
        values(2 * qi - 2, 0)
        probs(2 * qi, 0, 0)
        values(2 * qi - 1, 1)

    lam = (jnp.exp(jnp.sum(lq1_ref[...] * lk1_ref[...], axis=-1, keepdims=True))
           - jnp.exp(jnp.sum(lq2_ref[...] * lk2_ref[...], axis=-1, keepdims=True))
           + lam_init)

    def finish(rows):
        inv0 = 1.0 / jnp.sum(l_ref[0, rows, :], axis=-1, keepdims=True)
        inv1 = lam / jnp.sum(l_ref[1, rows, :], axis=-1, keepdims=True)
        o = acc_ref[0, rows, :] * inv0 - acc_ref[1, rows, :] * inv1
        r = lax.rsqrt(jnp.mean(o * o, axis=-1, keepdims=True) + EPS) * (1.0 - lam_init)
        y = (o * r) * (sg_ref[...] * g_ref[rows, :].astype(F32))
        o_ref[rows, :] = y.astype(o_ref.dtype)

    probs(2 * qi + 1, 1, TK)
    values(2 * qi, 0)
    finish(slice(0, TK))
    values(2 * qi + 1, 1, first_row=TK)
    finish(slice(TK, TQ))


def _attention(q, kt, vb, gates, lq1, lk1, lq2, lk2, subln_g, lam_init, batch, seq):
    t = q.shape[0]
    nq = seq // TQ
    nk = seq // TK
    hw = 2 * DA_HD
    vec = pl.BlockSpec((1, DA_HD), lambda b, h, i: (0, 0))
    return pl.pallas_call(
        functools.partial(_attn_kernel, lam_init),
        out_shape=jax.ShapeDtypeStruct((t, D_B), BF16),
        grid=(batch, DA_HEADS, nq),
        in_specs=[pl.BlockSpec((TQ, hw), lambda b, h, i: (b * nq + i, h)),
                  pl.BlockSpec((nk, hw, TK), lambda b, h, i: (b, h, 0)),
                  pl.BlockSpec((seq, hw), lambda b, h, i: (b, h)),
                  pl.BlockSpec((TQ, hw), lambda b, h, i: (b * nq + i, DA_HEADS + h)),
                  vec, vec, vec, vec,
                  pl.BlockSpec((1, hw), lambda b, h, i: (0, 0))],
        out_specs=pl.BlockSpec((TQ, hw), lambda b, h, i: (b * nq + i, h)),
        scratch_shapes=[pltpu.VMEM((2, 2, TQ, TK), BF16),
                        pltpu.VMEM((2, 2, TQ, LANES), F32),
                        pltpu.VMEM((2, 2, TQ, LANES), F32),
                        pltpu.VMEM((2, TQ, hw), F32),
                        pltpu.VMEM((2, TQ, LANES), F32),
                        pltpu.VMEM((2, TQ, LANES), F32)],
        compiler_params=_params("arbitrary", "arbitrary", "arbitrary"),
        name="diff_attention",
    )(q, kt, vb, gates, lq1.reshape(1, DA_HD), lk1.reshape(1, DA_HD),
      lq2.reshape(1, DA_HD), lk2.reshape(1, DA_HD), subln_g.reshape(1, hw))


def _outproj_kernel(ya_ref, yb_ref, wa_ref, wb_ref, x_ref, *refs):
    for r0 in range(0, BM, OUT_PIECE):
        rows = slice(r0, r0 + OUT_PIECE)
        acc = jnp.dot(ya_ref[rows, :], wa_ref[...], preferred_element_type=F32)
        acc = acc + jnp.dot(yb_ref[rows, :], wb_ref[...], preferred_element_type=F32)
        x_new = x_ref[rows, :] + acc
        if len(refs) == 1:
            (o_ref,) = refs
        else:
            g_ref, o_ref, xg_ref, ssq_ref = refs
            xg_ref[rows, :] = (x_new * g_ref[...]).astype(xg_ref.dtype)
            sq = x_new * x_new
            ssq_ref[rows, :] = functools.reduce(
                jnp.add, [sq[:, k * LANES:(k + 1) * LANES] for k in range(BN // LANES)])
        o_ref[rows, :] = x_new


def _outproj(ya, yb, w, layer, x, next_gain=None):
    t, d = x.shape
    in_specs = [pl.BlockSpec((BM, D_A), lambda j, i: (i, 0)),
                pl.BlockSpec((BM, D_B), lambda j, i: (i, 0)),
                pl.BlockSpec((None, D_A, BN), lambda j, i: (layer, 0, j)),
                pl.BlockSpec((None, D_B, BN), lambda j, i: (layer, 1, j)),
                pl.BlockSpec((BM, BN), lambda j, i: (i, j))]
    out_shape = jax.ShapeDtypeStruct((t, d), F32)
    out_specs = pl.BlockSpec((BM, BN), lambda j, i: (i, j))
    args = (ya, yb, w, w, x)
    if next_gain is not None:
        in_specs.append(pl.BlockSpec((1, BN), lambda j, i: (0, j)))
        args += (next_gain.reshape(1, d),)
        out_shape = (out_shape, jax.ShapeDtypeStruct((t, d), BF16),
                     jax.ShapeDtypeStruct((d // BN, t, LANES), F32))
        out_specs = (out_specs, pl.BlockSpec((BM, BN), lambda j, i: (i, j)),
                     pl.BlockSpec((None, BM, LANES), lambda j, i: (j, i, 0)))
    return pl.pallas_call(
        _outproj_kernel,
        out_shape=out_shape,
        grid=(d // BN, t // BM),
        in_specs=in_specs,
        out_specs=out_specs,
        compiler_params=_params("arbitrary", "arbitrary"),
        name="outproj",
    )(*args)


def _row_scale_kernel(ssq_ref, o_ref):
    total = jnp.sum(functools.reduce(jnp.add, [ssq_ref[k] for k in range(ssq_ref.shape[0])]),
                    axis=-1, keepdims=True)
    o_ref[...] = jnp.broadcast_to(lax.rsqrt(total * (1.0 / D_MODEL) + EPS), o_ref.shape)


def _row_scale(ssq):
    nparts, t, _ = ssq.shape
    rows = 2048
    return pl.pallas_call(
        _row_scale_kernel,
        out_shape=jax.ShapeDtypeStruct((t, LANES), F32),
        grid=(t // rows,),
        in_specs=[pl.BlockSpec((nparts, rows, LANES), lambda i: (0, i, 0))],
        out_specs=pl.BlockSpec((rows, LANES), lambda i: (i, 0)),
        compiler_params=_params("arbitrary"),
        name="row_scale",
    )(ssq)


def kernel(x, positions, norm_g, w_in, ln_g, ln_b, sgu_w, sgu_b, lam_q1, lam_k1, lam_q2, lam_k2,
           subln_g, w_out, final_g):
    batch, seq, d = x.shape
    t = batch * seq
    x = x.reshape(t, d)
    rope_c, rope_s1, rope_s2 = _rope_tables(positions)
    rope = (rope_c, rope_s1, rope_s2)
    w_out = w_out.astype(BF16)
    nb = D_A // BN
    h = _rmsnorm(x, norm_g[0], BF16, "rmsnorm_in")
    row_scale = None
    for l in range(DEPTH):
        guv = _inproj(h, row_scale, w_in, l, lambda j: j, 2 * nb,
                      functools.partial(_elementwise_epilogue, _gelu), name="inproj_gelu")
        gates = _inproj(h, row_scale, w_in, l, lambda j: jnp.where(j < nb, 2 * nb + j, 5 * nb + j), 2 * nb,
                        functools.partial(_elementwise_epilogue, _silu), name="inproj_silu")
        q = _inproj(h, row_scale, w_in, l, lambda j: 3 * nb + j, nb,
                    functools.partial(_rope_epilogue, DA_HD ** -0.5 * math.log2(math.e), False),
                    extra=rope, name="inproj_q")
        kt = _inproj(h, row_scale, w_in, l, lambda j: 4 * nb + j, nb,
                     functools.partial(_rope_epilogue, 1.0, True),
                     extra=rope, transposed=True, piece=256, name="inproj_k")
        vb = _inproj(h, row_scale, w_in, l, lambda j: 5 * nb + j, nb,
                     functools.partial(_elementwise_epilogue, _identity), name="inproj_v")
        ya = _sgu(guv, gates, ln_g[l], ln_b[l], sgu_w[l], sgu_b[l])
        lam_init = 0.8 - 0.6 * math.exp(-0.3 * l)
        yb = _attention(q, kt, vb, gates, lam_q1[l], lam_k1[l], lam_q2[l], lam_k2[l], subln_g[l],
                        lam_init, batch, seq)
        if l + 1 < DEPTH:
            x, h, ssq = _outproj(ya, yb, w_out, l, x, next_gain=norm_g[l + 1])
            row_scale = _row_scale(ssq)
        else:
            x = _outproj(ya, yb, w_out, l, x)
    out = _rmsnorm(x, final_g, F32, "rmsnorm_out")
    return out.reshape(batch, seq, d)
```

```python
import functools
import math

import jax
import jax.numpy as jnp
from jax import lax
from jax.experimental import pallas as pl
from jax.experimental.pallas import tpu as pltpu

D_MODEL = 4096
DEPTH = 2
CHUNK = 64
D_A = D_MODEL // 2
D_B = D_MODEL - D_A
SGU_BLOCK = 128
SGU_GROUPS = 8
SGU_GW = D_A // SGU_GROUPS
DA_HEADS = 8
DA_HD = D_B // DA_HEADS // 2
ROT_DIM = DA_HD // 4
ROPE_THETA = 500000.0
EPS = 1e-5

LANES = 128
VMEM_LIMIT = 56 * 1024 * 1024
NEG_BIG = -1e30

BM = 1024
BN = 1024
OUT_PIECE = 256
W_CHUNK = 512
W_PIECES = D_MODEL // W_CHUNK
NORM_ROWS = 512
SGU_ROWS = 512
TQ = 1024
TK = TQ // 2
ATT_ROWS = 128
PV_ROWS = 512
KEY_STEP = 256

F32 = jnp.float32
BF16 = jnp.bfloat16


def _params(*sem):
    return pltpu.CompilerParams(dimension_semantics=sem, vmem_limit_bytes=VMEM_LIMIT)


def _rmsnorm_kernel(x_ref, g_ref, o_ref):
    x = x_ref[...]
    r = lax.rsqrt(jnp.mean(x * x, axis=-1, keepdims=True) + EPS)
    o_ref[...] = (x * r * g_ref[...]).astype(o_ref.dtype)


def _rmsnorm(x, g, out_dtype, name):
    t, d = x.shape
    return pl.pallas_call(
        _rmsnorm_kernel,
        out_shape=jax.ShapeDtypeStruct((t, d), out_dtype),
        grid=(t // NORM_ROWS,),
        in_specs=[pl.BlockSpec((NORM_ROWS, d), lambda i: (i, 0)),
                  pl.BlockSpec((1, d), lambda i: (0, 0))],
        out_specs=pl.BlockSpec((NORM_ROWS, d), lambda i: (i, 0)),
        compiler_params=_params("arbitrary"),
        name=name,
    )(x, g.reshape(1, d))


def _rope_table_kernel(pos_ref, freq_ref, c_ref, s1_ref, s2_ref):
    ang = pos_ref[...] * freq_ref[...]
    cos = jnp.cos(ang)
    sin = jnp.sin(ang)
    lane = lax.broadcasted_iota(jnp.int32, ang.shape, 1)
    half = ROT_DIM // 2
    c_ref[...] = jnp.where(lane < ROT_DIM, cos, 1.0)
    s1_ref[...] = jnp.where(lane < half, -sin, 0.0)
    s2_ref[...] = jnp.where((lane >= half) & (lane < ROT_DIM), sin, 0.0)


def _rope_tables(positions):
    t = positions.size
    rows = 2048
    inv_freq = ROPE_THETA ** (-jnp.arange(0, ROT_DIM, 2, dtype=F32) / ROT_DIM)
    freq_row = jnp.zeros((1, LANES), F32).at[0, :ROT_DIM].set(jnp.tile(inv_freq, 2))
    pos = positions.astype(F32).reshape(t, 1)
    tab = jax.ShapeDtypeStruct((t, LANES), F32)
    spec = pl.BlockSpec((rows, LANES), lambda i: (i, 0))
    return pl.pallas_call(
        _rope_table_kernel,
        out_shape=(tab, tab, tab),
        grid=(t // rows,),
        in_specs=[pl.BlockSpec((rows, 1), lambda i: (i, 0)),
                  pl.BlockSpec((1, LANES), lambda i: (0, 0))],
        out_specs=(spec, spec, spec),
        compiler_params=_params("arbitrary"),
        name="rope_tables",
    )(pos, freq_row)


def _gelu(a):
    return 0.5 * a * (1.0 + lax.erf(a * (1.0 / math.sqrt(2.0))))


def _silu(a):
    return a / (1.0 + jnp.exp(-a))


def _identity(a):
    return a


def _stage_weights(w_hbm, layer, col_block, n_blocks, wbf_ref, stage_ref, sem_ref, part):
    j = pl.program_id(0)
    i = pl.program_id(1)

    def piece(block, c, slot):
        col = pl.multiple_of(col_block(block) * BN, BN)
        row = pl.multiple_of(c * W_CHUNK, W_CHUNK)
        return pltpu.make_async_copy(w_hbm.at[layer, pl.ds(row, W_CHUNK), pl.ds(col, BN)],
                                     stage_ref.at[slot], sem_ref.at[slot])

    def convert(block, c, slot):
        row = pl.multiple_of(c * W_CHUNK, W_CHUNK)
        wbf_ref[block % 2, pl.ds(row, W_CHUNK), :] = stage_ref[slot].astype(BF16)

    if part == "first":
        @pl.when((j == 0) & (i == 0))
        def _():
            piece(0, 0, 0).start()
            for c in range(W_PIECES):
                if c + 1 < W_PIECES:
                    piece(0, c + 1, (c + 1) % 2).start()
                piece(0, c, c % 2).wait()
                convert(0, c, c % 2)
        return

    @pl.when(j + 1 < n_blocks)
    def _():
        @pl.when((i >= 1) & (i <= W_PIECES))
        def _():
            piece(j + 1, i - 1, (i - 1) % 2).wait()
            convert(j + 1, i - 1, (i - 1) % 2)

        @pl.when(i < W_PIECES)
        def _():
            piece(j + 1, i, i % 2).start()


def _inproj_kernel(epilogue, layer, col_block, n_blocks, row_scaled, piece, x_ref, w_hbm, *refs):
    *extra, o_ref, wbf_ref, stage_ref, sem_ref = refs
    _stage_weights(w_hbm, layer, col_block, n_blocks, wbf_ref, stage_ref, sem_ref, "first")
    w = wbf_ref[pl.program_id(0) % 2]
    if row_scaled:
        r_ref, *extra = extra
    for r0 in range(0, BM, piece):
        rows = slice(r0, r0 + piece)
        acc = jnp.dot(x_ref[rows, :], w, preferred_element_type=F32)
        if row_scaled:
            acc = acc * jnp.concatenate([r_ref[rows, :]] * (BN // LANES), axis=-1)
        epilogue(acc, rows, *extra, o_ref)
    _stage_weights(w_hbm, layer, col_block, n_blocks, wbf_ref, stage_ref, sem_ref, "next")


def _elementwise_epilogue(fn, acc, rows, o_ref):
    o_ref[rows, :] = fn(acc).astype(o_ref.dtype)


def _rope_epilogue(scale, transposed, acc, rows, c_ref, s1_ref, s2_ref, o_ref):
    c = c_ref[rows, :] * scale
    s1 = s1_ref[rows, :] * scale
    s2 = s2_ref[rows, :] * scale
    half = ROT_DIM // 2
    for g in range(BN // LANES):
        cols = slice(g * LANES, (g + 1) * LANES)
        a = acc[:, cols]
        up = pltpu.roll(a, LANES - half, 1)
        down = pltpu.roll(a, half, 1)
        out = a * c + up * s1 + down * s2
        if transposed:
            lanes = slice(rows.start % TK, rows.start % TK + rows.stop - rows.start)
            o_ref[rows.start // TK, cols, lanes] = out.T.astype(o_ref.dtype)
        else:
            o_ref[rows, cols] = out.astype(o_ref.dtype)


def _inproj(h, row_scale, w, layer, col_block, n_blocks, epilogue, extra=(), transposed=False,
            piece=128, name=None):
    t, d = h.shape
    assert d == W_PIECES * W_CHUNK and t // BM > W_PIECES
    if row_scale is not None:
        extra = (row_scale,) + tuple(extra)
    extra_specs = [pl.BlockSpec((BM, LANES), lambda j, i: (i, 0)) for _ in extra]
    if transposed:
        out_shape = jax.ShapeDtypeStruct((t // TK, n_blocks * BN, TK), BF16)
        out_spec = pl.BlockSpec((BM // TK, BN, TK), lambda j, i: (i, j, 0))
    else:
        out_shape = jax.ShapeDtypeStruct((t, n_blocks * BN), BF16)
        out_spec = pl.BlockSpec((BM, BN), lambda j, i: (i, j))
    return pl.pallas_call(
        functools.partial(_inproj_kernel, epilogue, layer, col_block, n_blocks, row_scale is not None, piece),
        out_shape=out_shape,
        grid=(n_blocks, t // BM),
        in_specs=[pl.BlockSpec((BM, d), lambda j, i: (i, 0)),
                  pl.BlockSpec(memory_space=pl.ANY)] + extra_specs,
        out_specs=out_spec,
        scratch_shapes=[pltpu.VMEM((2, d, BN), BF16),
                        pltpu.VMEM((2, W_CHUNK, BN), F32),
                        pltpu.SemaphoreType.DMA((2,))],
        compiler_params=_params("arbitrary", "arbitrary"),
        name=name,
    )(h, w, *extra)


def _sgu_kernel(gu_ref, gv_ref, sg_ref, lng_ref, lnb_ref, w_ref, b_ref, o_ref):
    v = gv_ref[...].astype(F32)
    mu = jnp.mean(v, axis=-1, keepdims=True)
    xc = v - mu
    var = jnp.mean(xc * xc, axis=-1, keepdims=True)
    vn = (xc * lax.rsqrt(var + EPS) * lng_ref[...] + lnb_ref[...]).astype(BF16)
    row = lax.broadcasted_iota(jnp.int32, (SGU_BLOCK, SGU_BLOCK), 0)
    col = lax.broadcasted_iota(jnp.int32, (SGU_BLOCK, SGU_BLOCK), 1)
    mask = (col // CHUNK) <= (row // CHUNK)
    for g in range(SGU_GROUPS):
        wg = jnp.where(mask, w_ref[g], 0.0).astype(BF16)
        bg = b_ref[:, g:g + 1]
        cs = slice(g * SGU_GW, (g + 1) * SGU_GW)
        for r in range(SGU_ROWS // SGU_BLOCK):
            rs = slice(r * SGU_BLOCK, (r + 1) * SGU_BLOCK)
            mix = jnp.dot(wg, vn[rs, cs], preferred_element_type=F32) + bg
            y = gu_ref[rs, cs].astype(F32) * mix * sg_ref[rs, cs].astype(F32)
            o_ref[rs, cs] = y.astype(o_ref.dtype)


def _sgu(guv, sgate, ln_g, ln_b, sgu_w, sgu_b):
    t = guv.shape[0]
    return pl.pallas_call(
        _sgu_kernel,
        out_shape=jax.ShapeDtypeStruct((t, D_A), BF16),
        grid=(t // SGU_ROWS,),
        in_specs=[pl.BlockSpec((SGU_ROWS, D_A), lambda i: (i, 0)),
                  pl.BlockSpec((SGU_ROWS, D_A), lambda i: (i, 1)),
                  pl.BlockSpec((SGU_ROWS, D_A), lambda i: (i, 0)),
                  pl.BlockSpec((1, D_A), lambda i: (0, 0)),
                  pl.BlockSpec((1, D_A), lambda i: (0, 0)),
                  pl.BlockSpec((SGU_GROUPS, SGU_BLOCK, SGU_BLOCK), lambda i: (0, 0, 0)),
                  pl.BlockSpec((SGU_BLOCK, SGU_GROUPS), lambda i: (0, 0))],
        out_specs=pl.BlockSpec((SGU_ROWS, D_A), lambda i: (i, 0)),
        compiler_params=_params("arbitrary"),
        name="sgu",
    )(guv, guv, sgate, ln_g.reshape(1, D_A), ln_b.reshape(1, D_A), sgu_w, sgu_b.T)


def _attn_kernel(lam_init, q_ref, kt_ref, v_ref, g_ref, lq1_ref, lk1_ref, lq2_ref, lk2_ref, sg_ref,
                 o_ref, p_ref, d_ref, ps_ref, acc_ref, m_ref, l_ref):
    qi = pl.program_id(2)

    l_ref[...] = jnp.zeros(l_ref.shape, F32)
    acc_ref[...] = jnp.zeros(acc_ref.shape, F32)

    def probs(kb, slot, col0, first=False):
        for r0 in range(0, TQ, ATT_ROWS):
            if col0 is not None and r0 + ATT_ROWS <= col0:
                continue
            rows = slice(r0, r0 + ATT_ROWS)
            masked = col0 is not None and col0 + TK > r0
            n = TK
            if masked:
                n = min(TK, -(-(r0 + ATT_ROWS - col0) // KEY_STEP) * KEY_STEP)
                row = lax.broadcasted_iota(jnp.int32, (ATT_ROWS, n), 0) + r0
                col = lax.broadcasted_iota(jnp.int32, (ATT_ROWS, n), 1) + col0
                mask = (col // CHUNK) <= (row // CHUNK)
            for c in range(2):
                cs = slice(c * DA_HD, (c + 1) * DA_HD)
                s = jnp.dot(q_ref[rows, cs], kt_ref[kb, cs, :n], preferred_element_type=F32)
                if masked:
                    s = jnp.where(mask, s, NEG_BIG)
                if first:
                    m_prev = jnp.full((ATT_ROWS, LANES), NEG_BIG, F32)
                else:
                    m_prev = m_ref[c, rows, :]
                m_new = jnp.maximum(m_prev, jnp.max(s, axis=-1, keepdims=True))
                ps = [jnp.exp2(s[:, j * LANES:(j + 1) * LANES] - m_new) for j in range(n // LANES)]
                m_ref[c, rows, :] = m_new
                d_ref[slot, c, rows, :] = m_prev - m_new
                ps_ref[slot, c, rows, :] = functools.reduce(jnp.add, ps)
                p_ref[slot, c, rows, :n] = jnp.concatenate(ps, axis=-1).astype(BF16)
                if n < TK:
                    p_ref[slot, c, rows, n:] = jnp.zeros((ATT_ROWS, TK - n), BF16)

    def values(kb, slot, first_row=0):
        v_blk = v_ref[pl.ds(pl.multiple_of(kb * TK, TK), TK), :]
        for r0 in range(first_row, TQ, PV_ROWS):
            rows = slice(r0, r0 + PV_ROWS)
            for c in range(2):
                pv = jnp.dot(p_ref[slot, c, rows, :], v_blk, preferred_element_type=F32)
                alpha = jnp.exp2(d_ref[slot, c, rows, :])
                l_ref[c, rows, :] = alpha * l_ref[c, rows, :] + ps_ref[slot, c, rows, :]
                scale = jnp.concatenate([alpha] * (2 * DA_HD // LANES), axis=-1)
                acc_ref[c, rows, :] = scale * acc_ref[c, rows, :] + pv

    @pl.when(qi == 0)
    def _():
        probs(0, 0, 0, first=True)

    @pl.when(qi > 0)
    def _():
        probs(0, 0, None, first=True)

    def body(t, carry):
        probs(2 * t + 1, 1, None)
        values(2 * t, 0)
        probs(2 * t + 2, 0, None)
        values(2 * t + 1, 1)
        return carry

    lax.fori_loop(0, qi - 1, body, 0)

    @pl.when(qi > 0)
    def _():
        probs(2 * qi - 1, 1, None)
        values(2 * qi - 2, 0)
        probs(2 * qi, 0, 0)
        values(2 * qi - 1, 1)

    lam = (jnp.exp(jnp.sum(lq1_ref[...] * lk1_ref[...], axis=-1, keepdims=True))
           - jnp.exp(jnp.sum(lq2_ref[...] * lk2_ref[...], axis=-1, keepdims=True))
           + lam_init)

    def finish(rows):
        inv0 = 1.0 / jnp.sum(l_ref[0, rows, :], axis=-1, keepdims=True)
        inv1 = lam / jnp.sum(l_ref[1, rows, :], axis=-1, keepdims=True)
        o = acc_ref[0, rows, :] * inv0 - acc_ref[1, rows, :] * inv1
        r = lax.rsqrt(jnp.mean(o * o, axis=-1, keepdims=True) + EPS) * (1.0 - lam_init)
        y = (o * r) * (sg_ref[...] * g_ref[rows, :].astype(F32))
        o_ref[rows, :] = y.astype(o_ref.dtype)

    probs(2 * qi + 1, 1, TK)
    values(2 * qi, 0)
    finish(slice(0, TK))
    values(2 * qi + 1, 1, first_row=TK)
    finish(slice(TK, TQ))


def _attention(q, kt, vb, gates, lq1, lk1, lq2, lk2, subln_g, lam_init, batch, seq):
    t = q.shape[0]
    nq = seq // TQ
    nk = seq // TK
    hw = 2 * DA_HD
    vec = pl.BlockSpec((1, DA_HD), lambda b, h, i: (0, 0))
    return pl.pallas_call(
        functools.partial(_attn_kernel, lam_init),
        out_shape=jax.ShapeDtypeStruct((t, D_B), BF16),
        grid=(batch, DA_HEADS, nq),
        in_specs=[pl.BlockSpec((TQ, hw), lambda b, h, i: (b * nq + i, h)),
                  pl.BlockSpec((nk, hw, TK), lambda b, h, i: (b, h, 0)),
                  pl.BlockSpec((seq, hw), lambda b, h, i: (b, h)),
                  pl.BlockSpec((TQ, hw), lambda b, h, i: (b * nq + i, DA_HEADS + h)),
                  vec, vec, vec, vec,
                  pl.BlockSpec((1, hw), lambda b, h, i: (0, 0))],
        out_specs=pl.BlockSpec((TQ, hw), lambda b, h, i: (b * nq + i, h)),
        scratch_shapes=[pltpu.VMEM((2, 2, TQ, TK), BF16),
                        pltpu.VMEM((2, 2, TQ, LANES), F32),
                        pltpu.VMEM((2, 2, TQ, LANES), F32),
                        pltpu.VMEM((2, TQ, hw), F32),
                        pltpu.VMEM((2, TQ, LANES), F32),
                        pltpu.VMEM((2, TQ, LANES), F32)],
        compiler_params=_params("arbitrary", "arbitrary", "arbitrary"),
        name="diff_attention",
    )(q, kt, vb, gates, lq1.reshape(1, DA_HD), lk1.reshape(1, DA_HD),
      lq2.reshape(1, DA_HD), lk2.reshape(1, DA_HD), subln_g.reshape(1, hw))


def _outproj_kernel(ya_ref, yb_ref, wa_ref, wb_ref, x_ref, *refs):
    for r0 in range(0, BM, OUT_PIECE):
        rows = slice(r0, r0 + OUT_PIECE)
        acc = jnp.dot(ya_ref[rows, :], wa_ref[...], preferred_element_type=F32)
        acc = acc + jnp.dot(yb_ref[rows, :], wb_ref[...], preferred_element_type=F32)
        x_new = x_ref[rows, :] + acc
        if len(refs) == 1:
            (o_ref,) = refs
        else:
            g_ref, o_ref, xg_ref, ssq_ref = refs
            xg_ref[rows, :] = (x_new * g_ref[...]).astype(xg_ref.dtype)
            sq = x_new * x_new
            ssq_ref[rows, :] = functools.reduce(
                jnp.add, [sq[:, k * LANES:(k + 1) * LANES] for k in range(BN // LANES)])
        o_ref[rows, :] = x_new


def _outproj(ya, yb, w, layer, x, next_gain=None):
    t, d = x.shape
    in_specs = [pl.BlockSpec((BM, D_A), lambda j, i: (i, 0)),
                pl.BlockSpec((BM, D_B), lambda j, i: (i, 0)),
                pl.BlockSpec((None, D_A, BN), lambda j, i: (layer, 0, j)),
                pl.BlockSpec((None, D_B, BN), lambda j, i: (layer, 1, j)),
                pl.BlockSpec((BM, BN), lambda j, i: (i, j))]
    out_shape = jax.ShapeDtypeStruct((t, d), F32)
    out_specs = pl.BlockSpec((BM, BN), lambda j, i: (i, j))
    args = (ya, yb, w, w, x)
    if next_gain is not None:
        in_specs.append(pl.BlockSpec((1, BN), lambda j, i: (0, j)))
        args += (next_gain.reshape(1, d),)
        out_shape = (out_shape, jax.ShapeDtypeStruct((t, d), BF16),
                     jax.ShapeDtypeStruct((d // BN, t, LANES), F32))
        out_specs = (out_specs, pl.BlockSpec((BM, BN), lambda j, i: (i, j)),
                     pl.BlockSpec((None, BM, LANES), lambda j, i: (j, i, 0)))
    return pl.pallas_call(
        _outproj_kernel,
        out_shape=out_shape,
        grid=(d // BN, t // BM),
        in_specs=in_specs,
        out_specs=out_specs,
        compiler_params=_params("arbitrary", "arbitrary"),
        name="outproj",
    )(*args)


def _row_scale_kernel(ssq_ref, o_ref):
    total = jnp.sum(functools.reduce(jnp.add, [ssq_ref[k] for k in range(ssq_ref.shape[0])]),
                    axis=-1, keepdims=True)
    o_ref[...] = jnp.broadcast_to(lax.rsqrt(total * (1.0 / D_MODEL) + EPS), o_ref.shape)


def _row_scale(ssq):
    nparts, t, _ = ssq.shape
    rows = 2048
    return pl.pallas_call(
        _row_scale_kernel,
        out_shape=jax.ShapeDtypeStruct((t, LANES), F32),
        grid=(t // rows,),
        in_specs=[pl.BlockSpec((nparts, rows, LANES), lambda i: (0, i, 0))],
        out_specs=pl.BlockSpec((rows, LANES), lambda i: (i, 0)),
        compiler_params=_params("arbitrary"),
        name="row_scale",
    )(ssq)


def kernel(x, positions, norm_g, w_in, ln_g, ln_b, sgu_w, sgu_b, lam_q1, lam_k1, lam_q2, lam_k2,
           subln_g, w_out, final_g):
    batch, seq, d = x.shape
    t = batch * seq
    x = x.reshape(t, d)
    rope_c, rope_s1, rope_s2 = _rope_tables(positions)
    rope = (rope_c, rope_s1, rope_s2)
    w_out = w_out.astype(BF16)
    nb = D_A // BN
    h = _rmsnorm(x, norm_g[0], BF16, "rmsnorm_in")
    row_scale = None
    for l in range(DEPTH):
        guv = _inproj(h, row_scale, w_in, l, lambda j: j, 2 * nb,
                      functools.partial(_elementwise_epilogue, _gelu), name="inproj_gelu")
        gates = _inproj(h, row_scale, w_in, l, lambda j: jnp.where(j < nb, 2 * nb + j, 5 * nb + j), 2 * nb,
                        functools.partial(_elementwise_epilogue, _silu), name="inproj_silu")
        q = _inproj(h, row_scale, w_in, l, lambda j: 3 * nb + j, nb,
                    functools.partial(_rope_epilogue, DA_HD ** -0.5 * math.log2(math.e), False),
                    extra=rope, name="inproj_q")
        kt = _inproj(h, row_scale, w_in, l, lambda j: 4 * nb + j, nb,
                     functools.partial(_rope_epilogue, 1.0, True),
                     extra=rope, transposed=True, piece=256, name="inproj_k")
        vb = _inproj(h, row_scale, w_in, l, lambda j: 5 * nb + j, nb,
                     functools.partial(_elementwise_epilogue, _identity), name="inproj_v")
        ya = _sgu(guv, gates, ln_g[l], ln_b[l], sgu_w[l], sgu_b[l])
        lam_init = 0.8 - 0.6 * math.exp(-0.3 * l)
        yb = _attention(q, kt, vb, gates, lam_q1[l], lam_k1[l], lam_q2[l], lam_k2[l], subln_g[l],
                        lam_init, batch, seq)
        if l + 1 < DEPTH:
            x, h, ssq = _outproj(ya, yb, w_out, l, x, next_gain=norm_g[l + 1])
            row_scale = _row_scale(ssq)
        else:
            x = _outproj(ya, yb, w_out, l, x)
    out = _rmsnorm(x, final_g, F32, "rmsnorm_out")
    return out.reshape(batch, seq, d)
```

```python
import functools
import math

import jax
import jax.numpy as jnp
from jax import lax
from jax.experimental import pallas as pl
from jax.experimental.pallas import tpu as pltpu

D_MODEL = 4096
DEPTH = 2
CHUNK = 64
D_A = D_MODEL // 2
D_B = D_MODEL - D_A
SGU_BLOCK = 128
SGU_GROUPS = 8
SGU_GW = D_A // SGU_GROUPS
DA_HEADS = 8
DA_HD = D_B // DA_HEADS // 2
ROT_DIM = DA_HD // 4
ROPE_THETA = 500000.0
EPS = 1e-5

LANES = 128
VMEM_LIMIT = 56 * 1024 * 1024
NEG_BIG = -1e30

BM = 1024
BN = 1024
OUT_PIECE = 256
W_CHUNK = 512
W_PIECES = D_MODEL // W_CHUNK
NORM_ROWS = 512
SGU_ROWS = 512
TQ = 1024
TK = TQ // 2
ATT_ROWS = 128
PV_ROWS = 512
KEY_STEP = 256

F32 = jnp.float32
BF16 = jnp.bfloat16


def _params(*sem):
    return pltpu.CompilerParams(dimension_semantics=sem, vmem_limit_bytes=VMEM_LIMIT)


def _rmsnorm_kernel(x_ref, g_ref, o_ref):
    x = x_ref[...]
    r = lax.rsqrt(jnp.mean(x * x, axis=-1, keepdims=True) + EPS)
    o_ref[...] = (x * r * g_ref[...]).astype(o_ref.dtype)


def _rmsnorm(x, g, out_dtype, name):
    t, d = x.shape
    return pl.pallas_call(
        _rmsnorm_kernel,
        out_shape=jax.ShapeDtypeStruct((t, d), out_dtype),
        grid=(t // NORM_ROWS,),
        in_specs=[pl.BlockSpec((NORM_ROWS, d), lambda i: (i, 0)),
                  pl.BlockSpec((1, d), lambda i: (0, 0))],
        out_specs=pl.BlockSpec((NORM_ROWS, d), lambda i: (i, 0)),
        compiler_params=_params("arbitrary"),
        name=name,
    )(x, g.reshape(1, d))


def _rope_table_kernel(pos_ref, freq_ref, c_ref, s1_ref, s2_ref):
    ang = pos_ref[...] * freq_ref[...]
    cos = jnp.cos(ang)
    sin = jnp.sin(ang)
    lane = lax.broadcasted_iota(jnp.int32, ang.shape, 1)
    half = ROT_DIM // 2
    c_ref[...] = jnp.where(lane < ROT_DIM, cos, 1.0)
    s1_ref[...] = jnp.where(lane < half, -sin, 0.0)
    s2_ref[...] = jnp.where((lane >= half) & (lane < ROT_DIM), sin, 0.0)


def _rope_tables(positions):
    t = positions.size
    rows = 2048
    inv_freq = ROPE_THETA ** (-jnp.arange(0, ROT_DIM, 2, dtype=F32) / ROT_DIM)
    freq_row = jnp.zeros((1, LANES), F32).at[0, :ROT_DIM].set(jnp.tile(inv_freq, 2))
    pos = positions.astype(F32).reshape(t, 1)
    tab = jax.ShapeDtypeStruct((t, LANES), F32)
    spec = pl.BlockSpec((rows, LANES), lambda i: (i, 0))
    return pl.pallas_call(
        _rope_table_kernel,
        out_shape=(tab, tab, tab),
        grid=(t // rows,),
        in_specs=[pl.BlockSpec((rows, 1), lambda i: (i, 0)),
                  pl.BlockSpec((1, LANES), lambda i: (0, 0))],
        out_specs=(spec, spec, spec),
        compiler_params=_params("arbitrary"),
        name="rope_tables",
    )(pos, freq_row)


def _gelu(a):
    return 0.5 * a * (1.0 + lax.erf(a * (1.0 / math.sqrt(2.0))))


def _silu(a):
    half = 0.5 * a
    return half + half * jnp.tanh(half)


def _identity(a):
    return a


def _stage_weights(w_hbm, layer, col_block, n_blocks, wbf_ref, stage_ref, sem_ref, part):
    j = pl.program_id(0)
    i = pl.program_id(1)

    def piece(block, c, slot):
        col = pl.multiple_of(col_block(block) * BN, BN)
        row = pl.multiple_of(c * W_CHUNK, W_CHUNK)
        return pltpu.make_async_copy(w_hbm.at[layer, pl.ds(row, W_CHUNK), pl.ds(col, BN)],
                                     stage_ref.at[slot], sem_ref.at[slot])

    def convert(block, c, slot):
        row = pl.multiple_of(c * W_CHUNK, W_CHUNK)
        wbf_ref[block % 2, pl.ds(row, W_CHUNK), :] = stage_ref[slot].astype(BF16)

    if part == "first":
        @pl.when((j == 0) & (i == 0))
        def _():
            piece(0, 0, 0).start()
            for c in range(W_PIECES):
                if c + 1 < W_PIECES:
                    piece(0, c + 1, (c + 1) % 2).start()
                piece(0, c, c % 2).wait()
                convert(0, c, c % 2)
        return

    @pl.when(j + 1 < n_blocks)
    def _():
        @pl.when((i >= 1) & (i <= W_PIECES))
        def _():
            piece(j + 1, i - 1, (i - 1) % 2).wait()
            convert(j + 1, i - 1, (i - 1) % 2)

        @pl.when(i < W_PIECES)
        def _():
            piece(j + 1, i, i % 2).start()


def _inproj_kernel(epilogue, layer, col_block, n_blocks, row_scaled, piece, x_ref, w_hbm, *refs):
    *extra, o_ref, wbf_ref, stage_ref, sem_ref = refs
    _stage_weights(w_hbm, layer, col_block, n_blocks, wbf_ref, stage_ref, sem_ref, "first")
    w = wbf_ref[pl.program_id(0) % 2]
    if row_scaled:
        r_ref, *extra = extra
    for r0 in range(0, BM, piece):
        rows = slice(r0, r0 + piece)
        acc = jnp.dot(x_ref[rows, :], w, preferred_element_type=F32)
        if row_scaled:
            acc = acc * jnp.concatenate([r_ref[rows, :]] * (BN // LANES), axis=-1)
        epilogue(acc, rows, *extra, o_ref)
    _stage_weights(w_hbm, layer, col_block, n_blocks, wbf_ref, stage_ref, sem_ref, "next")


def _elementwise_epilogue(fn, acc, rows, o_ref):
    o_ref[rows, :] = fn(acc).astype(o_ref.dtype)


def _rope_epilogue(scale, transposed, acc, rows, c_ref, s1_ref, s2_ref, o_ref):
    c = c_ref[rows, :] * scale
    s1 = s1_ref[rows, :] * scale
    s2 = s2_ref[rows, :] * scale
    half = ROT_DIM // 2
    for g in range(BN // LANES):
        cols = slice(g * LANES, (g + 1) * LANES)
        a = acc[:, cols]
        up = pltpu.roll(a, LANES - half, 1)
        down = pltpu.roll(a, half, 1)
        out = a * c + up * s1 + down * s2
        if transposed:
            lanes = slice(rows.start % TK, rows.start % TK + rows.stop - rows.start)
            o_ref[rows.start // TK, cols, lanes] = out.T.astype(o_ref.dtype)
        else:
            o_ref[rows, cols] = out.astype(o_ref.dtype)


def _inproj(h, row_scale, w, layer, col_block, n_blocks, epilogue, extra=(), transposed=False,
            piece=128, name=None):
    t, d = h.shape
    assert d == W_PIECES * W_CHUNK and t // BM > W_PIECES
    if row_scale is not None:
        extra = (row_scale,) + tuple(extra)
    extra_specs = [pl.BlockSpec((BM, LANES), lambda j, i: (i, 0)) for _ in extra]
    if transposed:
        out_shape = jax.ShapeDtypeStruct((t // TK, n_blocks * BN, TK), BF16)
        out_spec = pl.BlockSpec((BM // TK, BN, TK), lambda j, i: (i, j, 0))
    else:
        out_shape = jax.ShapeDtypeStruct((t, n_blocks * BN), BF16)
        out_spec = pl.BlockSpec((BM, BN), lambda j, i: (i, j))
    return pl.pallas_call(
        functools.partial(_inproj_kernel, epilogue, layer, col_block, n_blocks, row_scale is not None, piece),
        out_shape=out_shape,
        grid=(n_blocks, t // BM),
        in_specs=[pl.BlockSpec((BM, d), lambda j, i: (i, 0)),
                  pl.BlockSpec(memory_space=pl.ANY)] + extra_specs,
        out_specs=out_spec,
        scratch_shapes=[pltpu.VMEM((2, d, BN), BF16),
                        pltpu.VMEM((2, W_CHUNK, BN), F32),
                        pltpu.SemaphoreType.DMA((2,))],
        compiler_params=_params("arbitrary", "arbitrary"),
        name=name,
    )(h, w, *extra)


def _sgu_kernel(gu_ref, gv_ref, sg_ref, lng_ref, lnb_ref, w_ref, b_ref, o_ref):
    v = gv_ref[...].astype(F32)
    mu = jnp.mean(v, axis=-1, keepdims=True)
    xc = v - mu
    var = jnp.mean(xc * xc, axis=-1, keepdims=True)
    vn = (xc * lax.rsqrt(var + EPS) * lng_ref[...] + lnb_ref[...]).astype(BF16)
    row = lax.broadcasted_iota(jnp.int32, (SGU_BLOCK, SGU_BLOCK), 0)
    col = lax.broadcasted_iota(jnp.int32, (SGU_BLOCK, SGU_BLOCK), 1)
    mask = (col // CHUNK) <= (row // CHUNK)
    for g in range(SGU_GROUPS):
        wg = jnp.where(mask, w_ref[g], 0.0).astype(BF16)
        bg = b_ref[:, g:g + 1]
        cs = slice(g * SGU_GW, (g + 1) * SGU_GW)
        for r in range(SGU_ROWS // SGU_BLOCK):
            rs = slice(r * SGU_BLOCK, (r + 1) * SGU_BLOCK)
            mix = jnp.dot(wg, vn[rs, cs], preferred_element_type=F32) + bg
            y = gu_ref[rs, cs].astype(F32) * mix * sg_ref[rs, cs].astype(F32)
            o_ref[rs, cs] = y.astype(o_ref.dtype)


def _sgu(guv, sgate, ln_g, ln_b, sgu_w, sgu_b):
    t = guv.shape[0]
    return pl.pallas_call(
        _sgu_kernel,
        out_shape=jax.ShapeDtypeStruct((t, D_A), BF16),
        grid=(t // SGU_ROWS,),
        in_specs=[pl.BlockSpec((SGU_ROWS, D_A), lambda i: (i, 0)),
                  pl.BlockSpec((SGU_ROWS, D_A), lambda i: (i, 1)),
                  pl.BlockSpec((SGU_ROWS, D_A), lambda i: (i, 0)),
                  pl.BlockSpec((1, D_A), lambda i: (0, 0)),
                  pl.BlockSpec((1, D_A), lambda i: (0, 0)),
                  pl.BlockSpec((SGU_GROUPS, SGU_BLOCK, SGU_BLOCK), lambda i: (0, 0, 0)),
                  pl.BlockSpec((SGU_BLOCK, SGU_GROUPS), lambda i: (0, 0))],
        out_specs=pl.BlockSpec((SGU_ROWS, D_A), lambda i: (i, 0)),
        compiler_params=_params("arbitrary"),
        name="sgu",
    )(guv, guv, sgate, ln_g.reshape(1, D_A), ln_b.reshape(1, D_A), sgu_w, sgu_b.T)


def _attn_kernel(lam_init, q_ref, kt_ref, v_ref, g_ref, lq1_ref, lk1_ref, lq2_ref, lk2_ref, sg_ref,
                 o_ref, p_ref, d_ref, ps_ref, acc_ref, m_ref, l_ref):
    qi = pl.program_id(2)

    l_ref[...] = jnp.zeros(l_ref.shape, F32)
    acc_ref[...] = jnp.zeros(acc_ref.shape, F32)

    def probs(kb, slot, col0, first=False):
        for r0 in range(0, TQ, ATT_ROWS):
            if col0 is not None and r0 + ATT_ROWS <= col0:
                continue
            rows = slice(r0, r0 + ATT_ROWS)
            masked = col0 is not None and col0 + TK > r0
            n = TK
            if masked:
                n = min(TK, -(-(r0 + ATT_ROWS - col0) // KEY_STEP) * KEY_STEP)
                row = lax.broadcasted_iota(jnp.int32, (ATT_ROWS, n), 0) + r0
                col = lax.broadcasted_iota(jnp.int32, (ATT_ROWS, n), 1) + col0
                mask = (col // CHUNK) <= (row // CHUNK)
            for c in range(2):
                cs = slice(c * DA_HD, (c + 1) * DA_HD)
                s = jnp.dot(q_ref[rows, cs], kt_ref[kb, cs, :n], preferred_element_type=F32)
                if masked:
                    s = jnp.where(mask, s, NEG_BIG)
                if first:
                    m_prev = jnp.full((ATT_ROWS, LANES), NEG_BIG, F32)
                else:
                    m_prev = m_ref[c, rows, :]
                m_new = jnp.maximum(m_prev, jnp.max(s, axis=-1, keepdims=True))
                ps = [jnp.exp2(s[:, j * LANES:(j + 1) * LANES] - m_new) for j in range(n // LANES)]
                m_ref[c, rows, :] = m_new
                d_ref[slot, c, rows, :] = m_prev - m_new
                ps_ref[slot, c, rows, :] = functools.reduce(jnp.add, ps)
                p_ref[slot, c, rows, :n] = jnp.concatenate(ps, axis=-1).astype(BF16)
                if n < TK:
                    p_ref[slot, c, rows, n:] = jnp.zeros((ATT_ROWS, TK - n), BF16)

    def values(kb, slot, first_row=0):
        v_blk = v_ref[pl.ds(pl.multiple_of(kb * TK, TK), TK), :]
        for r0 in range(first_row, TQ, PV_ROWS):
            rows = slice(r0, r0 + PV_ROWS)
            for c in range(2):
                pv = jnp.dot(p_ref[slot, c, rows, :], v_blk, preferred_element_type=F32)
                alpha = jnp.exp2(d_ref[slot, c, rows, :])
                l_ref[c, rows, :] = alpha * l_ref[c, rows, :] + ps_ref[slot, c, rows, :]
                scale = jnp.concatenate([alpha] * (2 * DA_HD // LANES), axis=-1)
                acc_ref[c, rows, :] = scale * acc_ref[c, rows, :] + pv

    @pl.when(qi == 0)
    def _():
        probs(0, 0, 0, first=True)

    @pl.when(qi > 0)
    def _():
        probs(0, 0, None, first=True)

    def body(t, carry):
        probs(2 * t + 1, 1, None)
        values(2 * t, 0)
        probs(2 * t + 2, 0, None)
        values(2 * t + 1, 1)
        return carry

    lax.fori_loop(0, qi - 1, body, 0)

    @pl.when(qi > 0)
    def _():
        probs(2 * qi - 1, 1, None)
        values(2 * qi - 2, 0)
        probs(2 * qi, 0, 0)
        values(2 * qi - 1, 1)

    lam = (jnp.exp(jnp.sum(lq1_ref[...] * lk1_ref[...], axis=-1, keepdims=True))
           - jnp.exp(jnp.sum(lq2_ref[...] * lk2_ref[...], axis=-1, keepdims=True))
           + lam_init)

    def finish(rows):
        inv0 = 1.0 / jnp.sum(l_ref[0, rows, :], axis=-1, keepdims=True)
        inv1 = lam / jnp.sum(l_ref[1, rows, :], axis=-1, keepdims=True)
        o = acc_ref[0, rows, :] * inv0 - acc_ref[1, rows, :] * inv1
        r = lax.rsqrt(jnp.mean(o * o, axis=-1, keepdims=True) + EPS) * (1.0 - lam_init)
        y = (o * r) * (sg_ref[...] * g_ref[rows, :].astype(F32))
        o_ref[rows, :] = y.astype(o_ref.dtype)

    probs(2 * qi + 1, 1, TK)
    values(2 * qi, 0)
    finish(slice(0, TK))
    values(2 * qi + 1, 1, first_row=TK)
    finish(slice(TK, TQ))


def _attention(q, kt, vb, gates, lq1, lk1, lq2, lk2, subln_g, lam_init, batch, seq):
    t = q.shape[0]
    nq = seq // TQ
    nk = seq // TK
    hw = 2 * DA_HD
    vec = pl.BlockSpec((1, DA_HD), lambda b, h, i: (0, 0))
    return pl.pallas_call(
        functools.partial(_attn_kernel, lam_init),
        out_shape=jax.ShapeDtypeStruct((t, D_B), BF16),
        grid=(batch, DA_HEADS, nq),
        in_specs=[pl.BlockSpec((TQ, hw), lambda b, h, i: (b * nq + i, h)),
                  pl.BlockSpec((nk, hw, TK), lambda b, h, i: (b, h, 0)),
                  pl.BlockSpec((seq, hw), lambda b, h, i: (b, h)),
                  pl.BlockSpec((TQ, hw), lambda b, h, i: (b * nq + i, DA_HEADS + h)),
                  vec, vec, vec, vec,
                  pl.BlockSpec((1, hw), lambda b, h, i: (0, 0))],
        out_specs=pl.BlockSpec((TQ, hw), lambda b, h, i: (b * nq + i, h)),
        scratch_shapes=[pltpu.VMEM((2, 2, TQ, TK), BF16),
                        pltpu.VMEM((2, 2, TQ, LANES), F32),
                        pltpu.VMEM((2, 2, TQ, LANES), F32),
                        pltpu.VMEM((2, TQ, hw), F32),
                        pltpu.VMEM((2, TQ, LANES), F32),
                        pltpu.VMEM((2, TQ, LANES), F32)],
        compiler_params=_params("arbitrary", "arbitrary", "arbitrary"),
        name="diff_attention",
    )(q, kt, vb, gates, lq1.reshape(1, DA_HD), lk1.reshape(1, DA_HD),
      lq2.reshape(1, DA_HD), lk2.reshape(1, DA_HD), subln_g.reshape(1, hw))


def _outproj_kernel(ya_ref, yb_ref, wa_ref, wb_ref, x_ref, *refs):
    for r0 in range(0, BM, OUT_PIECE):
        rows = slice(r0, r0 + OUT_PIECE)
        acc = jnp.dot(ya_ref[rows, :], wa_ref[...], preferred_element_type=F32)
        acc = acc + jnp.dot(yb_ref[rows, :], wb_ref[...], preferred_element_type=F32)
        x_new = x_ref[rows, :] + acc
        if len(refs) == 1:
            (o_ref,) = refs
        else:
            g_ref, o_ref, xg_ref, ssq_ref = refs
            xg_ref[rows, :] = (x_new * g_ref[...]).astype(xg_ref.dtype)
            sq = x_new * x_new
            ssq_ref[rows, :] = functools.reduce(
                jnp.add, [sq[:, k * LANES:(k + 1) * LANES] for k in range(BN // LANES)])
        o_ref[rows, :] = x_new


def _outproj(ya, yb, w, layer, x, next_gain=None):
    t, d = x.shape
    in_specs = [pl.BlockSpec((BM, D_A), lambda j, i: (i, 0)),
                pl.BlockSpec((BM, D_B), lambda j, i: (i, 0)),
                pl.BlockSpec((None, D_A, BN), lambda j, i: (layer, 0, j)),
                pl.BlockSpec((None, D_B, BN), lambda j, i: (layer, 1, j)),
                pl.BlockSpec((BM, BN), lambda j, i: (i, j))]
    out_shape = jax.ShapeDtypeStruct((t, d), F32)
    out_specs = pl.BlockSpec((BM, BN), lambda j, i: (i, j))
    args = (ya, yb, w, w, x)
    if next_gain is not None:
        in_specs.append(pl.BlockSpec((1, BN), lambda j, i: (0, j)))
        args += (next_gain.reshape(1, d),)
        out_shape = (out_shape, jax.ShapeDtypeStruct((t, d), BF16),
                     jax.ShapeDtypeStruct((d // BN, t, LANES), F32))
        out_specs = (out_specs, pl.BlockSpec((BM, BN), lambda j, i: (i, j)),
                     pl.BlockSpec((None, BM, LANES), lambda j, i: (j, i, 0)))
    return pl.pallas_call(
        _outproj_kernel,
        out_shape=out_shape,
        grid=(d // BN, t // BM),
        in_specs=in_specs,
        out_specs=out_specs,
        compiler_params=_params("arbitrary", "arbitrary"),
        name="outproj",
    )(*args)


def _row_scale_kernel(ssq_ref, o_ref):
    total = jnp.sum(functools.reduce(jnp.add, [ssq_ref[k] for k in range(ssq_ref.shape[0])]),
                    axis=-1, keepdims=True)
    o_ref[...] = jnp.broadcast_to(lax.rsqrt(total * (1.0 / D_MODEL) + EPS), o_ref.shape)


def _row_scale(ssq):
    nparts, t, _ = ssq.shape
    rows = 2048
    return pl.pallas_call(
        _row_scale_kernel,
        out_shape=jax.ShapeDtypeStruct((t, LANES), F32),
        grid=(t // rows,),
        in_specs=[pl.BlockSpec((nparts, rows, LANES), lambda i: (0, i, 0))],
        out_specs=pl.BlockSpec((rows, LANES), lambda i: (i, 0)),
        compiler_params=_params("arbitrary"),
        name="row_scale",
    )(ssq)


def kernel(x, positions, norm_g, w_in, ln_g, ln_b, sgu_w, sgu_b, lam_q1, lam_k1, lam_q2, lam_k2,
           subln_g, w_out, final_g):
    batch, seq, d = x.shape
    t = batch * seq
    x = x.reshape(t, d)
    rope_c, rope_s1, rope_s2 = _rope_tables(positions)
    rope = (rope_c, rope_s1, rope_s2)
    w_out = w_out.astype(BF16)
    nb = D_A // BN
    h = _rmsnorm(x, norm_g[0], BF16, "rmsnorm_in")
    row_scale = None
    for l in range(DEPTH):
        guv = _inproj(h, row_scale, w_in, l, lambda j: j, 2 * nb,
                      functools.partial(_elementwise_epilogue, _gelu), name="inproj_gelu")
        gates = _inproj(h, row_scale, w_in, l, lambda j: jnp.where(j < nb, 2 * nb + j, 5 * nb + j), 2 * nb,
                        functools.partial(_elementwise_epilogue, _silu), name="inproj_silu")
        q = _inproj(h, row_scale, w_in, l, lambda j: 3 * nb + j, nb,
                    functools.partial(_rope_epilogue, DA_HD ** -0.5 * math.log2(math.e), False),
                    extra=rope, name="inproj_q")
        kt = _inproj(h, row_scale, w_in, l, lambda j: 4 * nb + j, nb,
                     functools.partial(_rope_epilogue, 1.0, True),
                     extra=rope, transposed=True, piece=256, name="inproj_k")
        vb = _inproj(h, row_scale, w_in, l, lambda j: 5 * nb + j, nb,
                     functools.partial(_elementwise_epilogue, _identity), name="inproj_v")
        ya = _sgu(guv, gates, ln_g[l], ln_b[l], sgu_w[l], sgu_b[l])
        lam_init = 0.8 - 0.6 * math.exp(-0.3 * l)
        yb = _attention(q, kt, vb, gates, lam_q1[l], lam_k1[l], lam_q2[l], lam_k2[l], subln_g[l],
                        lam_init, batch, seq)
        if l + 1 < DEPTH:
            x, h, ssq = _outproj(ya, yb, w_out, l, x, next_gain=norm_g[l + 1])
            row_scale = _row_scale(ssq)
        else:
            x = _outproj(ya, yb, w_out, l, x)
    out = _rmsnorm(x, final_g, F32, "rmsnorm_out")
    return out.reshape(batch, seq, d)
```
